```python
import jax, jax.numpy as jnp
from jax import lax
import numpy as np

D_MODEL = 1024
BATCH = 2
SEQ = 8192
DEPTH = 1

N_META = 16
HGRN_DK = 128
HGRN_HEADS = D_MODEL // HGRN_DK
HGRN_DV = D_MODEL // HGRN_HEADS
HGRN_WIDTH = HGRN_HEADS * HGRN_DK
CHUNK = 64
POOL_WINDOWS = (2, 4, 8, 16)
POOL_GROUPS = len(POOL_WINDOWS)
POOL_GROUP_DIM = 128
POOL_WIDTH = POOL_GROUPS * POOL_GROUP_DIM
D_FF = -(-8 * D_MODEL // (3 * 256)) * 256
EPS = 1e-6
SPLIT_SIZES = (HGRN_WIDTH, HGRN_WIDTH, HGRN_HEADS * HGRN_DV, HGRN_HEADS * HGRN_DV,
               POOL_WIDTH, D_MODEL, D_MODEL)
SPLIT_POINTS = tuple(int(v) for v in np.cumsum(SPLIT_SIZES)[:-1])
IN_WIDTH = int(sum(SPLIT_SIZES))

kernel_name = "hgrn2_pool_gated_hybrid"


def rmsnorm(x, g):
    xf = x.astype(jnp.float32)
    y = xf * lax.rsqrt(jnp.mean(xf * xf, axis=-1, keepdims=True) + EPS)
    return (y * g.astype(jnp.float32)).astype(x.dtype)


def hgrn2_chunked(q, k, v, log_f):
    B, L, H, DK = q.shape
    DV = v.shape[-1]
    nc = L // CHUNK

    def to_chunks(t):
        return t.reshape(B, nc, CHUNK, H, t.shape[-1]).transpose(1, 0, 3, 2, 4)

    qc, kc, vc, gc = to_chunks(q), to_chunks(k), to_chunks(v), to_chunks(log_f)
    causal = jnp.tril(jnp.ones((CHUNK, CHUNK), dtype=bool))[None, None, :, :, None]

    def step(S, inp):
        qi, ki, vi, gi = inp
        b = jnp.cumsum(gi, axis=2)
        b_last = b[:, :, -1:, :]
        o_inter = jnp.einsum('bhtd,bhde->bhte', qi * jnp.exp(b), S)
        diff = b[:, :, :, None, :] - b[:, :, None, :, :]
        decay = jnp.exp(jnp.where(causal, diff, -jnp.inf))
        scores = jnp.einsum('bhtd,bhsd,bhtsd->bhts', qi, ki, decay)
        o_intra = jnp.einsum('bhts,bhse->bhte', scores, vi)
        S_new = (jnp.exp(b_last)[:, :, 0, :, None] * S
                 + jnp.einsum('bhsd,bhse->bhde', ki * jnp.exp(b_last - b), vi))
        return S_new, o_inter + o_intra

    S0 = jnp.zeros((B, H, DK, DV), dtype=jnp.float32)
    _, o = lax.scan(step, S0, (qc, kc, vc, gc))
    return o.transpose(1, 0, 3, 2, 4).reshape(B, L, H, DV)


def causal_multiscale_pool(u):
    B, L, _ = u.shape
    uf = u.astype(jnp.float32).reshape(B, L, POOL_GROUPS, POOL_GROUP_DIM)
    cs = jnp.cumsum(uf, axis=1)
    pos = jnp.arange(1, L + 1, dtype=jnp.float32)
    outs = []
    for gi, w in enumerate(POOL_WINDOWS):
        c = cs[:, :, gi]
        shifted = jnp.pad(c, ((0, 0), (w, 0), (0, 0)))[:, :L]
        cnt = jnp.minimum(pos, float(w))[None, :, None]
        outs.append((c - shifted) / cnt - uf[:, :, gi])
    return jnp.stack(outs, axis=2)


def setup_inputs(seed: int = 0) -> dict:
    key = jax.random.key(seed)
    ks = jax.random.split(key, 17)
    f32 = jnp.float32

    def w(k, shape, fan_in):
        return jax.random.normal(k, shape, f32) * (fan_in ** -0.5)

    def gain(k, shape):
        return 1.0 + 0.02 * jax.random.normal(k, shape, f32)

    return {
        "x": jax.random.normal(ks[0], (BATCH, SEQ, D_MODEL), f32),
        "meta_tokens": jax.random.normal(ks[1], (N_META, D_MODEL), f32),
        "norm_mix_g": gain(ks[2], (DEPTH, D_MODEL)),
        "w_in": w(ks[3], (DEPTH, D_MODEL, IN_WIDTH), D_MODEL),
        "lb_raw": 0.5 * jax.random.normal(ks[4], (DEPTH + 1, HGRN_WIDTH), f32),
        "hgrn_norm_g": gain(ks[5], (DEPTH, HGRN_DV)),
        "pool_w": w(ks[6], (DEPTH, POOL_GROUPS, POOL_GROUP_DIM, POOL_GROUP_DIM), POOL_GROUP_DIM),
        "pool_scale": gain(ks[7], (DEPTH, POOL_WIDTH)),
        "w_branch_a": w(ks[8], (DEPTH, HGRN_HEADS * HGRN_DV, D_MODEL), HGRN_HEADS * HGRN_DV),
        "w_branch_b": w(ks[9], (DEPTH, POOL_WIDTH, D_MODEL), POOL_WIDTH),
        "w_out": w(ks[10], (DEPTH, D_MODEL, D_MODEL), D_MODEL),
        "norm_ffn_g": gain(ks[11], (DEPTH, D_MODEL)),
        "w_ffn_gate": w(ks[12], (DEPTH, D_MODEL, D_FF), D_MODEL),
        "w_ffn_up": w(ks[13], (DEPTH, D_MODEL, D_FF), D_MODEL),
        "w_ffn_down": w(ks[14], (DEPTH, D_FF, D_MODEL), D_FF),
        "norm_final_g": gain(ks[15], (D_MODEL,)),
    }


def reference(x, meta_tokens, norm_mix_g, w_in, lb_raw, hgrn_norm_g, pool_w, pool_scale,
              w_branch_a, w_branch_b, w_out, norm_ffn_g, w_ffn_gate, w_ffn_up, w_ffn_down,
              norm_final_g):
    f32 = jnp.float32
    B = x.shape[0]
    meta = jnp.broadcast_to(meta_tokens[None].astype(x.dtype), (B, N_META, D_MODEL))
    h = jnp.concatenate([meta, x], axis=1)
    L = h.shape[1]
    pad = CHUNK - N_META
    lb_all = jnp.cumsum(jax.nn.softmax(lb_raw.astype(f32), axis=0), axis=0)

    def front_pad_heads(t):
        return jnp.pad(t, ((0, 0), (pad, 0), (0, 0))).reshape(B, pad + L, HGRN_HEADS, -1)

    for l in range(DEPTH):
        n = rmsnorm(h, norm_mix_g[l])
        proj = n @ w_in[l]
        zq, zf, zi, zog, zpool, zga, zgb = jnp.split(proj, SPLIT_POINTS, axis=-1)

        lb = lb_all[l]
        q = jax.nn.silu(zq.astype(f32))
        f = lb + (1.0 - lb) * jax.nn.sigmoid(zf.astype(f32))
        k = 1.0 - f
        log_f = jnp.log(f)
        v = zi.astype(f32)
        o = hgrn2_chunked(front_pad_heads(q), front_pad_heads(k),
                          front_pad_heads(v), front_pad_heads(log_f))[:, pad:]
        o = rmsnorm(o, hgrn_norm_g[l]).reshape(B, L, HGRN_HEADS * HGRN_DV)
        y_a = (o * jax.nn.silu(zog.astype(f32))).astype(h.dtype)

        pooled = causal_multiscale_pool(zpool)
        y_b = jnp.einsum('blgc,gcd->blgd', pooled, pool_w[l].astype(f32)).reshape(B, L, POOL_WIDTH)
        y_b = (y_b * pool_scale[l].astype(f32)).astype(h.dtype)

        merged = (jax.nn.sigmoid(zga) * (y_a @ w_branch_a[l])
                  + jax.nn.sigmoid(zgb) * (y_b @ w_branch_b[l]))
        h = h + merged @ w_out[l]

        n2 = rmsnorm(h, norm_ffn_g[l])
        h = h + (jax.nn.silu(n2 @ w_ffn_gate[l]) * (n2 @ w_ffn_up[l])) @ w_ffn_down[l]

    return rmsnorm(h, norm_final_g)[:, N_META:]
```

```python
import functools

import jax
import jax.numpy as jnp
from jax import lax
from jax.experimental import pallas as pl
from jax.experimental.pallas import tpu as pltpu

D_MODEL = 1024
N_META = 16
HEADS = 8
HEAD_DIM = 128
HGRN_WIDTH = HEADS * HEAD_DIM
POOL_WINDOWS = (2, 4, 8, 16)
POOL_GROUP_DIM = 128
POOL_WIDTH = len(POOL_WINDOWS) * POOL_GROUP_DIM
POOL_HISTORY = 16
EPS = 1e-6

OFF_Q = 0
OFF_F = OFF_Q + HGRN_WIDTH
OFF_I = OFF_F + HGRN_WIDTH
OFF_OG = OFF_I + HGRN_WIDTH
OFF_POOL = OFF_OG + HGRN_WIDTH
OFF_GA = OFF_POOL + POOL_WIDTH
OFF_GB = OFF_GA + D_MODEL
IN_WIDTH = OFF_GB + D_MODEL

SUBLANES = 8
MIXER_BLOCK = 256
FFN_BLOCK = 512
VMEM_LIMIT_BYTES = 56 * 1024 * 1024

F32 = jnp.float32
BF16 = jnp.bfloat16


def _rmsnorm(x, g):
    ms = jnp.mean(x * x, axis=-1, keepdims=True)
    return x * lax.rsqrt(ms + EPS) * g


def _dot(a, b):
    return jnp.dot(a, b, preferred_element_type=F32)


def _dot_nt(a, b):
    return lax.dot_general(a, b, (((1,), (1,)), ((), ())), preferred_element_type=F32)


def _dot_tn(a, b):
    return lax.dot_general(a, b, (((0,), (0,)), ((), ())), preferred_element_type=F32)


def _lower_bound(lb_raw):
    r0, r1 = lb_raw[0:1, :], lb_raw[1:2, :]
    m = jnp.maximum(r0, r1)
    e0, e1 = jnp.exp(r0 - m), jnp.exp(r1 - m)
    return e0 / (e0 + e1)


def _cumsum_rows(g, tri):
    g1 = g.astype(BF16)
    rem = g - g1.astype(F32)
    g2 = rem.astype(BF16)
    g3 = (rem - g2.astype(F32)).astype(BF16)
    return _dot(tri, g1) + _dot(tri, g2) + _dot(tri, g3)


def _tri(n):
    r = lax.broadcasted_iota(jnp.int32, (n, n), 0)
    c = lax.broadcasted_iota(jnp.int32, (n, n), 1)
    return (c <= r).astype(BF16)


def _forget_gate(zf, lb):
    f = lb + (1.0 - lb) * jax.nn.sigmoid(zf)
    return 1.0 - f, jnp.log(f)


def _meta_state_kernel(meta_ref, g_ref, wf_ref, wi_ref, wp_ref, lb_raw_ref,
                       state_ref, hist_ref):
    n = _rmsnorm(meta_ref[...], g_ref[...]).astype(BF16)
    lb = _lower_bound(lb_raw_ref[...])
    k, g = _forget_gate(_dot(n, wf_ref[...]), lb)
    v = _dot(n, wi_ref[...])
    hist_ref[...] = _dot(n, wp_ref[...])
    b = _cumsum_rows(g, _tri(N_META))
    k_end = (k * jnp.exp(b[N_META - 1:N_META, :] - b)).astype(BF16)
    v = v.astype(BF16)
    for h in range(HEADS):
        sl = slice(h * HEAD_DIM, (h + 1) * HEAD_DIM)
        state_ref[h] = _dot_tn(v[:, sl], k_end[:, sl])


def _hgrn_head(q, k, v, b, state_t, row, pair_masks):
    T = q.shape[0]
    v_bf = v.astype(BF16)

    o = _dot_nt((q * jnp.exp(b)).astype(BF16), state_t.astype(BF16))

    sub = row % SUBLANES
    o = o + jnp.sum(q * k, axis=-1, keepdims=True) * v
    for lag in range(1, SUBLANES):
        valid = sub >= lag
        k_l = pltpu.roll(k, lag, 0)
        b_l = pltpu.roll(b, lag, 0)
        v_l = pltpu.roll(v, lag, 0)
        w = jnp.exp(jnp.where(valid, b - b_l, 0.0))
        p = jnp.where(valid, q * k_l * w, 0.0)
        o = o + jnp.sum(p, axis=-1, keepdims=True) * v_l

    scores = None
    m = SUBLANES
    while m < T:
        nb = T // (2 * m)
        r = b.reshape(nb, 2 * m, HEAD_DIM)[:, m - 1:m, :]
        r = jnp.broadcast_to(r, (nb, 2 * m, HEAD_DIM)).reshape(T, HEAD_DIM)
        later = ((row // m) % 2) == 1
        e = jnp.exp(jnp.where(later, b - r, r - b))
        q_l = jnp.where(later, q * e, 0.0).astype(BF16)
        k_l = jnp.where(later, 0.0, k * e).astype(BF16)
        s = _dot_nt(q_l, k_l)
        if 2 * m < T:
            s = jnp.where(pair_masks[m], s, 0.0)
        scores = s if scores is None else scores + s
        m *= 2
    o = o + _dot(scores.astype(BF16), v_bf)

    b_end = b[T - 1:T, :]
    k_end = (k * jnp.exp(b_end - b)).astype(BF16)
    new_state_t = state_t * jnp.exp(b_end) + _dot_tn(v_bf, k_end)
    return o, new_state_t


def _mixer_kernel(x_ref, g_mix_ref, w_in_ref, lb_raw_ref, g_hgrn_ref, pool_w_ref, pool_scale_ref,
                  w_a_ref, w_b_ref, w_out_ref, state0_ref, hist0_ref,
                  out_ref, proj_ref, ya_ref, yb_ref, state_ref, hist_ref):
    T = MIXER_BLOCK

    @pl.when(pl.program_id(1) == 0)
    def _():
        state_ref[...] = state0_ref[...]
        hist_ref[...] = hist0_ref[...]

    x = x_ref[0]
    n = _rmsnorm(x, g_mix_ref[...]).astype(BF16)
    proj_ref[...] = _dot(n, w_in_ref[...])

    lb = _lower_bound(lb_raw_ref[...])
    k_all, g_all = _forget_gate(proj_ref[:, OFF_F:OFF_F + HGRN_WIDTH], lb)
    b_all = _cumsum_rows(g_all, _tri(T))
    row = lax.broadcasted_iota(jnp.int32, (T, HEAD_DIM), 0)
    r_i = lax.broadcasted_iota(jnp.int32, (T, T), 0)
    c_i = lax.broadcasted_iota(jnp.int32, (T, T), 1)
    pair_masks = {}
    m = SUBLANES
    while 2 * m < T:
        pair_masks[m] = (r_i // (2 * m)) == (c_i // (2 * m))
        m *= 2
    g_hgrn = g_hgrn_ref[...]
    for h in range(HEADS):
        sl = slice(h * HEAD_DIM, (h + 1) * HEAD_DIM)
        q = jax.nn.silu(proj_ref[:, OFF_Q + h * HEAD_DIM:OFF_Q + (h + 1) * HEAD_DIM])
        v = proj_ref[:, OFF_I + h * HEAD_DIM:OFF_I + (h + 1) * HEAD_DIM]
        o, new_state = _hgrn_head(q, k_all[:, sl], v, b_all[:, sl], state_ref[h], row, pair_masks)
        state_ref[h] = new_state
        og = jax.nn.silu(proj_ref[:, OFF_OG + h * HEAD_DIM:OFF_OG + (h + 1) * HEAD_DIM])
        ya_ref[:, sl] = (_rmsnorm(o, g_hgrn) * og).astype(BF16)

    u = proj_ref[:, OFF_POOL:OFF_POOL + POOL_WIDTH]
    ext = jnp.concatenate([hist_ref[...], u], axis=0)
    hist_ref[...] = u[T - POOL_HISTORY:, :]
    acc = ext
    span = 1
    for gi, w in enumerate(POOL_WINDOWS):
        while span < w:
            acc = acc + pltpu.roll(acc, span, 0)
            span *= 2
        sl = slice(gi * POOL_GROUP_DIM, (gi + 1) * POOL_GROUP_DIM)
        pooled = acc[POOL_HISTORY:, sl] * (1.0 / w) - u[:, sl]
        y = _dot(pooled.astype(BF16), pool_w_ref[gi]) * pool_scale_ref[:, sl]
        yb_ref[:, sl] = y.astype(BF16)

    merged = (jax.nn.sigmoid(proj_ref[:, OFF_GA:OFF_GA + D_MODEL]) * _dot(ya_ref[...], w_a_ref[...])
              + jax.nn.sigmoid(proj_ref[:, OFF_GB:OFF_GB + D_MODEL]) * _dot(yb_ref[...], w_b_ref[...]))
    out_ref[0] = x + _dot(merged.astype(BF16), w_out_ref[...])


def _ffn_kernel(h_ref, g_ffn_ref, w_gate_ref, w_up_ref, w_down_ref, g_final_ref, out_ref):
    h = h_ref[...]
    n = _rmsnorm(h, g_ffn_ref[...]).astype(BF16)
    act = jax.nn.silu(_dot(n, w_gate_ref[...])) * _dot(n, w_up_ref[...])
    h = h + _dot(act.astype(BF16), w_down_ref[...])
    out_ref[...] = _rmsnorm(h, g_final_ref[...])


def _resident(shape):
    zeros = (0,) * len(shape)
    return pl.BlockSpec(shape, lambda *_: zeros, pipeline_mode=pl.Buffered(1))


def kernel(x, meta_tokens, norm_mix_g, w_in, lb_raw, hgrn_norm_g, pool_w, pool_scale,
           w_branch_a, w_branch_b, w_out, norm_ffn_g, w_ffn_gate, w_ffn_up, w_ffn_down,
           norm_final_g):
    B, S, D = x.shape
    assert D == D_MODEL and S % MIXER_BLOCK == 0 and (B * S) % FFN_BLOCK == 0
    assert w_in.shape == (1, D_MODEL, IN_WIDTH) and meta_tokens.shape == (N_META, D_MODEL)
    d_ff = w_ffn_gate.shape[-1]

    w_in_bf = w_in[0].astype(BF16)
    g_mix = norm_mix_g[0].reshape(1, D)
    lb_raw = lb_raw.astype(F32)

    state0, hist0 = pl.pallas_call(
        _meta_state_kernel,
        out_shape=(jax.ShapeDtypeStruct((HEADS, HEAD_DIM, HEAD_DIM), F32),
                   jax.ShapeDtypeStruct((N_META, POOL_WIDTH), F32)),
        compiler_params=pltpu.CompilerParams(vmem_limit_bytes=VMEM_LIMIT_BYTES),
        name="meta_state",
    )(meta_tokens, g_mix,
      w_in_bf[:, OFF_F:OFF_F + HGRN_WIDTH], w_in_bf[:, OFF_I:OFF_I + HGRN_WIDTH],
      w_in_bf[:, OFF_POOL:OFF_POOL + POOL_WIDTH], lb_raw)

    T = MIXER_BLOCK
    h1 = pl.pallas_call(
        _mixer_kernel,
        grid=(B, S // T),
        in_specs=[
            pl.BlockSpec((1, T, D), lambda b, t: (b, t, 0)),
            _resident((1, D)),
            _resident((D, IN_WIDTH)),
            _resident((2, HGRN_WIDTH)),
            _resident((1, HEAD_DIM)),
            _resident((len(POOL_WINDOWS), POOL_GROUP_DIM, POOL_GROUP_DIM)),
            _resident((1, POOL_WIDTH)),
            _resident((HGRN_WIDTH, D)),
            _resident((POOL_WIDTH, D)),
            _resident((D, D)),
            _resident((HEADS, HEAD_DIM, HEAD_DIM)),
            _resident((N_META, POOL_WIDTH)),
        ],
        out_specs=pl.BlockSpec((1, T, D), lambda b, t: (b, t, 0)),
        out_shape=jax.ShapeDtypeStruct((B, S, D), F32),
        scratch_shapes=[
            pltpu.VMEM((T, IN_WIDTH), F32),
            pltpu.VMEM((T, HGRN_WIDTH), BF16),
            pltpu.VMEM((T, POOL_WIDTH), BF16),
            pltpu.VMEM((HEADS, HEAD_DIM, HEAD_DIM), F32),
            pltpu.VMEM((POOL_HISTORY, POOL_WIDTH), F32),
        ],
        compiler_params=pltpu.CompilerParams(
            dimension_semantics=("arbitrary", "arbitrary"),
            vmem_limit_bytes=VMEM_LIMIT_BYTES),
        name="mixer",
    )(x, g_mix, w_in_bf, lb_raw, hgrn_norm_g[0].reshape(1, HEAD_DIM),
      pool_w[0].astype(BF16), pool_scale[0].reshape(1, POOL_WIDTH),
      w_branch_a[0].astype(BF16), w_branch_b[0].astype(BF16), w_out[0].astype(BF16),
      state0, hist0)

    TM = FFN_BLOCK
    out = pl.pallas_call(
        _ffn_kernel,
        grid=(B * S // TM,),
        in_specs=[
            pl.BlockSpec((TM, D), lambda i: (i, 0)),
            _resident((1, D)),
            _resident((D, d_ff)),
            _resident((D, d_ff)),
            _resident((d_ff, D)),
            _resident((1, D)),
        ],
        out_specs=pl.BlockSpec((TM, D), lambda i: (i, 0)),
        out_shape=jax.ShapeDtypeStruct((B * S, D), F32),
        compiler_params=pltpu.CompilerParams(
            dimension_semantics=("arbitrary",),
            vmem_limit_bytes=VMEM_LIMIT_BYTES),
        name="ffn",
    )(h1.reshape(B * S, D), norm_ffn_g[0].reshape(1, D),
      w_ffn_gate[0].astype(BF16), w_ffn_up[0].astype(BF16), w_ffn_down[0].astype(BF16),
      norm_final_g.reshape(1, D))
    return out.reshape(B, S, D)
```

```python
import jax
import jax.numpy as jnp
from jax import lax
from jax.experimental import pallas as pl
from jax.experimental.pallas import tpu as pltpu

D_MODEL = 1024
N_META = 16
HEADS = 8
HEAD_DIM = 128
HGRN_WIDTH = HEADS * HEAD_DIM
POOL_WINDOWS = (2, 4, 8, 16)
POOL_GROUP_DIM = 128
POOL_WIDTH = len(POOL_WINDOWS) * POOL_GROUP_DIM
POOL_HISTORY = 16
EPS = 1e-6

OFF_Q = 0
OFF_F = OFF_Q + HGRN_WIDTH
OFF_I = OFF_F + HGRN_WIDTH
OFF_OG = OFF_I + HGRN_WIDTH
OFF_POOL = OFF_OG + HGRN_WIDTH
OFF_GA = OFF_POOL + POOL_WIDTH
OFF_GB = OFF_GA + D_MODEL
IN_WIDTH = OFF_GB + D_MODEL

SUBLANES = 8
MIXER_BLOCK = 256
SCORE_CHUNK = 128
FFN_BLOCK = 512
VMEM_LIMIT_BYTES = 56 * 1024 * 1024

F32 = jnp.float32
BF16 = jnp.bfloat16


def _rmsnorm(x, g):
    ms = jnp.mean(x * x, axis=-1, keepdims=True)
    return x * lax.rsqrt(ms + EPS) * g


def _dot(a, b):
    return jnp.dot(a, b, preferred_element_type=F32)


def _dot_nt(a, b):
    return lax.dot_general(a, b, (((1,), (1,)), ((), ())), preferred_element_type=F32)


def _dot_tn(a, b):
    return lax.dot_general(a, b, (((0,), (0,)), ((), ())), preferred_element_type=F32)


def _lower_bound(lb_raw):
    r0, r1 = lb_raw[0:1, :], lb_raw[1:2, :]
    m = jnp.maximum(r0, r1)
    e0, e1 = jnp.exp(r0 - m), jnp.exp(r1 - m)
    return e0 / (e0 + e1)


def _cumsum_rows(g, tri):
    g1 = g.astype(BF16)
    rem = g - g1.astype(F32)
    g2 = rem.astype(BF16)
    g3 = (rem - g2.astype(F32)).astype(BF16)
    return _dot(tri, g1) + _dot(tri, g2) + _dot(tri, g3)


def _tri(n):
    r = lax.broadcasted_iota(jnp.int32, (n, n), 0)
    c = lax.broadcasted_iota(jnp.int32, (n, n), 1)
    return (c <= r).astype(BF16)


def _forget_gate(zf, lb):
    f = lb + (1.0 - lb) * jax.nn.sigmoid(zf)
    return 1.0 - f, f, jnp.log2(f)


def _meta_state_kernel(meta_ref, g_ref, wf_ref, wi_ref, wp_ref, lb_raw_ref,
                       state_ref, hist_ref):
    n = _rmsnorm(meta_ref[...], g_ref[...]).astype(BF16)
    lb = _lower_bound(lb_raw_ref[...])
    k, _, g = _forget_gate(_dot(n, wf_ref[...]), lb)
    v = _dot(n, wi_ref[...])
    hist_ref[...] = _dot(n, wp_ref[...])
    b = _cumsum_rows(g, _tri(N_META))
    k_end = (k * jnp.exp2(b[N_META - 1:N_META, :] - b)).astype(BF16)
    v = v.astype(BF16)
    for h in range(HEADS):
        sl = slice(h * HEAD_DIM, (h + 1) * HEAD_DIM)
        state_ref[h] = _dot_tn(v[:, sl], k_end[:, sl])


def _sibling_boundary(b, m, sub):
    T = b.shape[0]
    if m >= SUBLANES:
        nb = T // (2 * m)
        r = b.reshape(nb, 2 * m, HEAD_DIM)[:, m - 1:m, :]
        return jnp.broadcast_to(r, (nb, 2 * m, HEAD_DIM)).reshape(T, HEAD_DIM)
    groups = b.reshape(T // SUBLANES, SUBLANES, HEAD_DIM)

    def bcast(i):
        r = jnp.broadcast_to(groups[:, i:i + 1, :], groups.shape)
        return r.reshape(T, HEAD_DIM)

    r = bcast(m - 1)
    for first in range(2 * m, SUBLANES, 2 * m):
        r = jnp.where(sub >= first, bcast(first + m - 1), r)
    return r


def _level_constants(T):
    row = lax.broadcasted_iota(jnp.int32, (T, HEAD_DIM), 0)
    r_i = lax.broadcasted_iota(jnp.int32, (SCORE_CHUNK, SCORE_CHUNK), 0)
    c_i = lax.broadcasted_iota(jnp.int32, (SCORE_CHUNK, SCORE_CHUNK), 1)
    consts = {}
    m = 1
    while m < T:
        later = ((row // m) % 2) == 1
        sign = jnp.where(later, 1.0, -1.0)
        mask = None
        if m < SCORE_CHUNK:
            mask = (((r_i // (2 * m)) == (c_i // (2 * m)))
                    & ((r_i // m) % 2 == 1) & ((c_i // m) % 2 == 0))
        consts[m] = (later, sign, mask)
        m *= 2
    return row % SUBLANES, consts


def _hgrn_head(q, k, f, v, b, state_t, sub, consts):
    T = q.shape[0]
    n_chunks = T // SCORE_CHUNK
    v_bf = v.astype(BF16)

    o = _dot_nt((q * jnp.exp2(b)).astype(BF16), state_t.astype(BF16))
    o = o + jnp.sum(q * k, axis=-1, keepdims=True) * v

    blocks = [[None] * n_chunks for _ in range(n_chunks)]
    m = 1
    while m < T:
        later, sign, mask = consts[m]
        if m == 1:
            z = jnp.where(later, q * f, k)
        else:
            r = _sibling_boundary(b, m, sub)
            z = jnp.where(later, q, k) * jnp.exp2((b - r) * sign)
        z = z.astype(BF16)
        zc = [z[c * SCORE_CHUNK:(c + 1) * SCORE_CHUNK] for c in range(n_chunks)]
        if m < SCORE_CHUNK:
            for c in range(n_chunks):
                s = _dot_nt(zc[c], zc[c])
                blocks[c][c] = jnp.where(mask, s, 0.0 if blocks[c][c] is None else blocks[c][c])
        else:
            span = m // SCORE_CHUNK
            for i in range(n_chunks):
                if (i // span) % 2 == 1:
                    for j in range((i // span - 1) * span, (i // span) * span):
                        blocks[i][j] = _dot_nt(zc[i], zc[j])
        m *= 2
    strips = []
    for i in range(n_chunks):
        a = jnp.concatenate(blocks[i][:i + 1], axis=1) if i else blocks[0][0]
        strips.append(_dot(a.astype(BF16), v_bf[:(i + 1) * SCORE_CHUNK]))
    o = o + (jnp.concatenate(strips, axis=0) if n_chunks > 1 else strips[0])

    b_end = b[T - 1:T, :]
    k_end = (k * jnp.exp2(b_end - b)).astype(BF16)
    new_state_t = state_t * jnp.exp2(b_end) + _dot_tn(v_bf, k_end)
    return o, new_state_t


def _mixer_kernel(x_ref, g_mix_ref, w_in_ref, lb_raw_ref, g_hgrn_ref, pool_w_ref, pool_scale_ref,
                  w_a_ref, w_b_ref, w_out_ref, state0_ref, hist0_ref,
                  out_ref, proj_ref, ya_ref, yb_ref, state_ref, hist_ref):
    T = MIXER_BLOCK

    @pl.when(pl.program_id(1) == 0)
    def _():
        state_ref[...] = state0_ref[...]
        hist_ref[...] = hist0_ref[...]

    x = x_ref[0]
    n = _rmsnorm(x, g_mix_ref[...]).astype(BF16)
    proj_ref[...] = _dot(n, w_in_ref[...])

    lb = _lower_bound(lb_raw_ref[...])
    k_all, f_all, g_all = _forget_gate(proj_ref[:, OFF_F:OFF_F + HGRN_WIDTH], lb)
    b_all = _cumsum_rows(g_all, _tri(T))
    sub, consts = _level_constants(T)
    g_hgrn = g_hgrn_ref[...]
    for h in range(HEADS):
        sl = slice(h * HEAD_DIM, (h + 1) * HEAD_DIM)
        q = jax.nn.silu(proj_ref[:, OFF_Q + h * HEAD_DIM:OFF_Q + (h + 1) * HEAD_DIM])
        v = proj_ref[:, OFF_I + h * HEAD_DIM:OFF_I + (h + 1) * HEAD_DIM]
        o, new_state = _hgrn_head(q, k_all[:, sl], f_all[:, sl], v, b_all[:, sl], state_ref[h],
                                  sub, consts)
        state_ref[h] = new_state
        og = jax.nn.silu(proj_ref[:, OFF_OG + h * HEAD_DIM:OFF_OG + (h + 1) * HEAD_DIM])
        ya_ref[:, sl] = (_rmsnorm(o, g_hgrn) * og).astype(BF16)

    u = proj_ref[:, OFF_POOL:OFF_POOL + POOL_WIDTH]
    ext = jnp.concatenate([hist_ref[...], u], axis=0)
    hist_ref[...] = u[T - POOL_HISTORY:, :]
    acc = ext
    span = 1
    for gi, w in enumerate(POOL_WINDOWS):
        while span < w:
            acc = acc + pltpu.roll(acc, span, 0)
            span *= 2
        sl = slice(gi * POOL_GROUP_DIM, (gi + 1) * POOL_GROUP_DIM)
        pooled = acc[POOL_HISTORY:, sl] * (1.0 / w) - u[:, sl]
        y = _dot(pooled.astype(BF16), pool_w_ref[gi]) * pool_scale_ref[:, sl]
        yb_ref[:, sl] = y.astype(BF16)

    merged = (jax.nn.sigmoid(proj_ref[:, OFF_GA:OFF_GA + D_MODEL]) * _dot(ya_ref[...], w_a_ref[...])
              + jax.nn.sigmoid(proj_ref[:, OFF_GB:OFF_GB + D_MODEL]) * _dot(yb_ref[...], w_b_ref[...]))
    out_ref[0] = x + _dot(merged.astype(BF16), w_out_ref[...])


def _ffn_kernel(h_ref, g_ffn_ref, w_gate_ref, w_up_ref, w_down_ref, g_final_ref, out_ref):
    h = h_ref[...]
    n = _rmsnorm(h, g_ffn_ref[...]).astype(BF16)
    act = jax.nn.silu(_dot(n, w_gate_ref[...])) * _dot(n, w_up_ref[...])
    h = h + _dot(act.astype(BF16), w_down_ref[...])
    out_ref[...] = _rmsnorm(h, g_final_ref[...])


def _resident(shape):
    zeros = (0,) * len(shape)
    return pl.BlockSpec(shape, lambda *_: zeros, pipeline_mode=pl.Buffered(1))


def kernel(x, meta_tokens, norm_mix_g, w_in, lb_raw, hgrn_norm_g, pool_w, pool_scale,
           w_branch_a, w_branch_b, w_out, norm_ffn_g, w_ffn_gate, w_ffn_up, w_ffn_down,
           norm_final_g):
    B, S, D = x.shape
    assert D == D_MODEL and S % MIXER_BLOCK == 0 and (B * S) % FFN_BLOCK == 0
    assert w_in.shape == (1, D_MODEL, IN_WIDTH) and meta_tokens.shape == (N_META, D_MODEL)
    d_ff = w_ffn_gate.shape[-1]

    w_in_bf = w_in[0].astype(BF16)
    g_mix = norm_mix_g[0].reshape(1, D)
    lb_raw = lb_raw.astype(F32)

    state0, hist0 = pl.pallas_call(
        _meta_state_kernel,
        out_shape=(jax.ShapeDtypeStruct((HEADS, HEAD_DIM, HEAD_DIM), F32),
                   jax.ShapeDtypeStruct((N_META, POOL_WIDTH), F32)),
        compiler_params=pltpu.CompilerParams(vmem_limit_bytes=VMEM_LIMIT_BYTES),
        name="meta_state",
    )(meta_tokens, g_mix,
      w_in_bf[:, OFF_F:OFF_F + HGRN_WIDTH], w_in_bf[:, OFF_I:OFF_I + HGRN_WIDTH],
      w_in_bf[:, OFF_POOL:OFF_POOL + POOL_WIDTH], lb_raw)

    T = MIXER_BLOCK
    h1 = pl.pallas_call(
        _mixer_kernel,
        grid=(B, S // T),
        in_specs=[
            pl.BlockSpec((1, T, D), lambda b, t: (b, t, 0)),
            _resident((1, D)),
            _resident((D, IN_WIDTH)),
            _resident((2, HGRN_WIDTH)),
            _resident((1, HEAD_DIM)),
            _resident((len(POOL_WINDOWS), POOL_GROUP_DIM, POOL_GROUP_DIM)),
            _resident((1, POOL_WIDTH)),
            _resident((HGRN_WIDTH, D)),
            _resident((POOL_WIDTH, D)),
            _resident((D, D)),
            _resident((HEADS, HEAD_DIM, HEAD_DIM)),
            _resident((N_META, POOL_WIDTH)),
        ],
        out_specs=pl.BlockSpec((1, T, D), lambda b, t: (b, t, 0)),
        out_shape=jax.ShapeDtypeStruct((B, S, D), F32),
        scratch_shapes=[
            pltpu.VMEM((T, IN_WIDTH), F32),
            pltpu.VMEM((T, HGRN_WIDTH), BF16),
            pltpu.VMEM((T, POOL_WIDTH), BF16),
            pltpu.VMEM((HEADS, HEAD_DIM, HEAD_DIM), F32),
            pltpu.VMEM((POOL_HISTORY, POOL_WIDTH), F32),
        ],
        compiler_params=pltpu.CompilerParams(
            dimension_semantics=("arbitrary", "arbitrary"),
            vmem_limit_bytes=VMEM_LIMIT_BYTES),
        name="mixer",
    )(x, g_mix, w_in_bf, lb_raw, hgrn_norm_g[0].reshape(1, HEAD_DIM),
      pool_w[0].astype(BF16), pool_scale[0].reshape(1, POOL_WIDTH),
      w_branch_a[0].astype(BF16), w_branch_b[0].astype(BF16), w_out[0].astype(BF16),
      state0, hist0)

    TM = FFN_BLOCK
    out = pl.pallas_call(
        _ffn_kernel,
        grid=(B * S // TM,),
        in_specs=[
            pl.BlockSpec((TM, D), lambda i: (i, 0)),
            _resident((1, D)),
            _resident((D, d_ff)),
            _resident((D, d_ff)),
            _resident((d_ff, D)),
            _resident((1, D)),
        ],
        out_specs=pl.BlockSpec((TM, D), lambda i: (i, 0)),
        out_shape=jax.ShapeDtypeStruct((B * S, D), F32),
        compiler_params=pltpu.CompilerParams(
            dimension_semantics=("arbitrary",),
            vmem_limit_bytes=VMEM_LIMIT_BYTES),
        name="ffn",
    )(h1.reshape(B * S, D), norm_ffn_g[0].reshape(1, D),
      w_ffn_gate[0].astype(BF16), w_ffn_up[0].astype(BF16), w_ffn_down[0].astype(BF16),
      norm_final_g.reshape(1, D))
    return out.reshape(B, S, D)
```

```python
import jax
import jax.numpy as jnp
from jax import lax
from jax.experimental import pallas as pl
from jax.experimental.pallas import tpu as pltpu

D_MODEL = 1024
N_META = 16
HEADS = 8
HEAD_DIM = 128
HGRN_WIDTH = HEADS * HEAD_DIM
POOL_WINDOWS = (2, 4, 8, 16)
POOL_GROUP_DIM = 128
POOL_WIDTH = len(POOL_WINDOWS) * POOL_GROUP_DIM
POOL_HISTORY = 16
EPS = 1e-6

OFF_Q = 0
OFF_F = OFF_Q + HGRN_WIDTH
OFF_I = OFF_F + HGRN_WIDTH
OFF_OG = OFF_I + HGRN_WIDTH
OFF_POOL = OFF_OG + HGRN_WIDTH
OFF_GA = OFF_POOL + POOL_WIDTH
OFF_GB = OFF_GA + D_MODEL
IN_WIDTH = OFF_GB + D_MODEL

SUBLANES = 8
MIXER_BLOCK = 256
SCORE_CHUNK = 128
DENSE_COLS = 256
MID_LEVEL = 8
FFN_BLOCK = 512
VMEM_LIMIT_BYTES = 56 * 1024 * 1024

F32 = jnp.float32
BF16 = jnp.bfloat16


def _rmsnorm(x, g):
    ms = jnp.mean(x * x, axis=-1, keepdims=True)
    return x * lax.rsqrt(ms + EPS) * g


def _dot(a, b):
    return jnp.dot(a, b, preferred_element_type=F32)


def _dot_nt(a, b):
    return lax.dot_general(a, b, (((1,), (1,)), ((), ())), preferred_element_type=F32)


def _dot_tn(a, b):
    return lax.dot_general(a, b, (((0,), (0,)), ((), ())), preferred_element_type=F32)


def _lower_bound(lb_raw):
    r0, r1 = lb_raw[0:1, :], lb_raw[1:2, :]
    m = jnp.maximum(r0, r1)
    e0, e1 = jnp.exp(r0 - m), jnp.exp(r1 - m)
    return e0 / (e0 + e1)


def _cumsum_rows(g, tri):
    g1 = g.astype(BF16)
    rem = g - g1.astype(F32)
    g2 = rem.astype(BF16)
    g3 = (rem - g2.astype(F32)).astype(BF16)
    return _dot(tri, g1) + _dot(tri, g2) + _dot(tri, g3)


def _tri(n):
    r = lax.broadcasted_iota(jnp.int32, (n, n), 0)
    c = lax.broadcasted_iota(jnp.int32, (n, n), 1)
    return (c <= r).astype(BF16)


def _forget_gate(zf, lb):
    f = lb + (1.0 - lb) * jax.nn.sigmoid(zf)
    return 1.0 - f, f, jnp.log2(f)


def _meta_state_kernel(meta_ref, g_ref, wf_ref, wi_ref, wp_ref, lb_raw_ref,
                       state_ref, hist_ref):
    n = _rmsnorm(meta_ref[...], g_ref[...]).astype(BF16)
    lb = _lower_bound(lb_raw_ref[...])
    k, _, g = _forget_gate(_dot(n, wf_ref[...]), lb)
    v = _dot(n, wi_ref[...])
    hist_ref[...] = _dot(n, wp_ref[...])
    b = _cumsum_rows(g, _tri(N_META))
    k_end = (k * jnp.exp2(b[N_META - 1:N_META, :] - b)).astype(BF16)
    v = v.astype(BF16)
    for h in range(HEADS):
        sl = slice(h * HEAD_DIM, (h + 1) * HEAD_DIM)
        state_ref[h] = _dot_tn(v[:, sl], k_end[:, sl])


def _sibling_boundary(b, m, sub):
    T = b.shape[0]
    if m >= SUBLANES:
        nb = T // (2 * m)
        r = b.reshape(nb, 2 * m, HEAD_DIM)[:, m - 1:m, :]
        return jnp.broadcast_to(r, (nb, 2 * m, HEAD_DIM)).reshape(T, HEAD_DIM)
    groups = b.reshape(T // SUBLANES, SUBLANES, HEAD_DIM)

    def bcast(i):
        r = jnp.broadcast_to(groups[:, i:i + 1, :], groups.shape)
        return r.reshape(T, HEAD_DIM)

    r = bcast(m - 1)
    for first in range(2 * m, SUBLANES, 2 * m):
        r = jnp.where(sub >= first, bcast(first + m - 1), r)
    return r


def _level_constants(T):
    row = lax.broadcasted_iota(jnp.int32, (T, HEAD_DIM), 0)
    r_i = lax.broadcasted_iota(jnp.int32, (SCORE_CHUNK, SCORE_CHUNK), 0)
    c_i = lax.broadcasted_iota(jnp.int32, (SCORE_CHUNK, SCORE_CHUNK), 1)
    consts = {}
    m = 1
    while m < T:
        later = ((row // m) % 2) == 1
        sign = jnp.where(later, 1.0, -1.0)
        mask = None
        if m < SCORE_CHUNK:
            mask = (((r_i // (2 * m)) == (c_i // (2 * m)))
                    & ((r_i // m) % 2 == 1) & ((c_i // m) % 2 == 0))
        consts[m] = (later, sign, mask)
        m *= 2
    return row % SUBLANES, consts


def _hgrn_scores(q, k, f, b, state_t, sub, consts, tick):
    T = q.shape[0]
    n_chunks = T // SCORE_CHUNK
    o_prev = _dot_nt((q * jnp.exp2(b)).astype(BF16), state_t.astype(BF16))

    blocks = [[None] * n_chunks for _ in range(n_chunks)]
    m = 1
    while m < T:
        later, sign, mask = consts[m]
        if m == 1:
            z = jnp.where(later, q * f, k)
        else:
            r = _sibling_boundary(b, m, sub)
            z = jnp.where(later, q, k) * jnp.exp2((b - r) * sign)
        z = z.astype(BF16)
        zc = [z[c * SCORE_CHUNK:(c + 1) * SCORE_CHUNK] for c in range(n_chunks)]
        if m < SCORE_CHUNK:
            for c in range(n_chunks):
                s = _dot_nt(zc[c], zc[c])
                blocks[c][c] = jnp.where(mask, s, 0.0 if blocks[c][c] is None else blocks[c][c])
        else:
            span = m // SCORE_CHUNK
            for i in range(n_chunks):
                if (i // span) % 2 == 1:
                    for j in range((i // span - 1) * span, (i // span) * span):
                        blocks[i][j] = _dot_nt(zc[i], zc[j])
        if m == MID_LEVEL:
            tick()
        m *= 2
    return o_prev, blocks


def _hgrn_output(o_prev, blocks, q, k, v, b, state_t):
    T = q.shape[0]
    n_chunks = T // SCORE_CHUNK
    v_bf = v.astype(BF16)
    strips = []
    for i in range(n_chunks):
        a = jnp.concatenate(blocks[i][:i + 1], axis=1) if i else blocks[0][0]
        strips.append(_dot(a.astype(BF16), v_bf[:(i + 1) * SCORE_CHUNK]))
    o = o_prev + (jnp.concatenate(strips, axis=0) if n_chunks > 1 else strips[0])
    o = o + jnp.sum(q * k, axis=-1, keepdims=True) * v

    b_end = b[T - 1:T, :]
    k_end = (k * jnp.exp2(b_end - b)).astype(BF16)
    new_state_t = state_t * jnp.exp2(b_end) + _dot_tn(v_bf, k_end)
    return o, new_state_t


def _mixer_kernel(x_ref, g_mix_ref, w_in_ref, lb_raw_ref, g_hgrn_ref, pool_w_ref, pool_scale_ref,
                  w_a_ref, w_b_ref, w_out_ref, state0_ref, hist0_ref,
                  out_ref, state_ref, hist_ref):
    T = MIXER_BLOCK
    B = x_ref.shape[0]

    @pl.when(pl.program_id(0) == 0)
    def _():
        for b in range(B):
            for h in range(HEADS):
                state_ref[b * HEADS + h] = state0_ref[h]
            hist_ref[b] = hist0_ref[...]

    xs = [x_ref[b] for b in range(B)]
    ns = [_rmsnorm(x, g_mix_ref[...]).astype(BF16) for x in xs]
    lb = _lower_bound(lb_raw_ref[...])
    tri = _tri(T)
    sub, consts = _level_constants(T)
    g_hgrn = g_hgrn_ref[...]
    PAIR = 2 * HEAD_DIM

    queue = []

    def tick(n=1):
        for _ in range(n):
            if queue:
                queue.pop(0)[1]()

    def flush(tag=None):
        while queue if tag is None else any(t == tag for t, _ in queue):
            queue.pop(0)[1]()

    def enqueue_dot(tag, lhs, rhs_ref, col0, n_cols):
        blocks = [None] * (n_cols // DENSE_COLS)
        for cb in range(len(blocks)):
            def piece(cb=cb):
                cols = slice(col0 + cb * DENSE_COLS, col0 + (cb + 1) * DENSE_COLS)
                blocks[cb] = _dot(lhs(), rhs_ref[:, cols])
            queue.append((tag, piece))
        return blocks

    def assemble(blocks):
        return jnp.concatenate(blocks, axis=1) if len(blocks) > 1 else blocks[0]

    def enqueue_projection(tag, b, offset, width):
        return enqueue_dot(tag, lambda: ns[b], w_in_ref, offset, width)

    def enqueue_pair(b, p):
        return tuple(enqueue_projection(("pair", b, p), b, off + p * PAIR, PAIR)
                     for off in (OFF_Q, OFF_F, OFF_I, OFF_OG))

    def hgrn_pair(b, p, zq, zf, zi, zog):
        k, f, g = _forget_gate(zf, lb[:, p * PAIR:(p + 1) * PAIR])
        bsum = _cumsum_rows(g, tri)
        tick()
        q = jax.nn.silu(zq)
        og = jax.nn.silu(zog)
        heads = [(b * HEADS + 2 * p + j, slice(j * HEAD_DIM, (j + 1) * HEAD_DIM))
                 for j in range(2)]
        scores = []
        for h, sl in heads:
            scores.append(_hgrn_scores(q[:, sl], k[:, sl], f[:, sl], bsum[:, sl], state_ref[h],
                                       sub, consts, tick))
            tick()
        tick()
        ys = []
        for (h, sl), (o_prev, blocks) in zip(heads, scores):
            o, new_state = _hgrn_output(o_prev, blocks, q[:, sl], k[:, sl], zi[:, sl],
                                        bsum[:, sl], state_ref[h])
            state_ref[h] = new_state
            ys.append((_rmsnorm(o, g_hgrn) * og[:, sl]).astype(BF16))
        return jnp.concatenate(ys, axis=1)

    def pool_branch(b, u):
        ext = jnp.concatenate([hist_ref[b], u], axis=0)
        hist_ref[b] = u[T - POOL_HISTORY:, :]
        acc = ext
        span = 1
        ys = []
        for gi, w in enumerate(POOL_WINDOWS):
            while span < w:
                acc = acc + pltpu.roll(acc, span, 0)
                span *= 2
            sl = slice(gi * POOL_GROUP_DIM, (gi + 1) * POOL_GROUP_DIM)
            pooled = acc[POOL_HISTORY:, sl] * (1.0 / w) - u[:, sl]
            y = _dot(pooled.astype(BF16), pool_w_ref[gi]) * pool_scale_ref[:, sl]
            ys.append(y.astype(BF16))
        return jnp.concatenate(ys, axis=1)

    def enqueue_outputs(b, ya, z_pool, z_gb, z_ga):
        n_blocks = D_MODEL // DENSE_COLS
        hold = {}

        def start():
            hold["ya"] = jnp.concatenate(ya, axis=1)
            hold["yb"] = pool_branch(b, assemble(z_pool))
        queue.append((("out", b), start))
        branch_a = enqueue_dot(("out", b), lambda: hold["ya"], w_a_ref, 0, D_MODEL)
        branch_b = enqueue_dot(("out", b), lambda: hold["yb"], w_b_ref, 0, D_MODEL)

        def merge():
            merged = [jax.nn.sigmoid(z_ga[cb]) * branch_a[cb]
                      + jax.nn.sigmoid(z_gb[cb]) * branch_b[cb] for cb in range(n_blocks)]
            hold["merged"] = jnp.concatenate(merged, axis=1).astype(BF16)
        queue.append((("out", b), merge))
        for cb in range(n_blocks):
            def piece(cb=cb):
                cols = slice(cb * DENSE_COLS, (cb + 1) * DENSE_COLS)
                out_ref[b, :, cols] = xs[b][:, cols] + _dot(hold["merged"], w_out_ref[:, cols])
            queue.append((("out", b), piece))

    n_pairs = HEADS // 2
    z = {(0, 0): enqueue_pair(0, 0)}
    flush()
    others = {}
    for b in range(B):
        for p in range(1, n_pairs):
            z[(b, p)] = enqueue_pair(b, p)
        if b + 1 < B:
            z[(b + 1, 0)] = enqueue_pair(b + 1, 0)
        others[b] = tuple(enqueue_projection(("other", b), b, off, width) for off, width in
                          ((OFF_POOL, POOL_WIDTH), (OFF_GB, D_MODEL), (OFF_GA, D_MODEL)))
    for b in range(B):
        ya = []
        for p in range(n_pairs):
            flush(("pair", b, p))
            ya.append(hgrn_pair(b, p, *(assemble(blocks) for blocks in z[(b, p)])))
        enqueue_outputs(b, ya, *others[b])
    flush()


def _ffn_kernel(h_ref, g_ffn_ref, w_gate_ref, w_up_ref, w_down_ref, g_final_ref, out_ref):
    h = h_ref[...]
    n = _rmsnorm(h, g_ffn_ref[...]).astype(BF16)
    act = jax.nn.silu(_dot(n, w_gate_ref[...])) * _dot(n, w_up_ref[...])
    h = h + _dot(act.astype(BF16), w_down_ref[...])
    out_ref[...] = _rmsnorm(h, g_final_ref[...])


def _resident(shape):
    zeros = (0,) * len(shape)
    return pl.BlockSpec(shape, lambda *_: zeros, pipeline_mode=pl.Buffered(1))


def kernel(x, meta_tokens, norm_mix_g, w_in, lb_raw, hgrn_norm_g, pool_w, pool_scale,
           w_branch_a, w_branch_b, w_out, norm_ffn_g, w_ffn_gate, w_ffn_up, w_ffn_down,
           norm_final_g):
    B, S, D = x.shape
    assert D == D_MODEL and S % MIXER_BLOCK == 0 and (B * S) % FFN_BLOCK == 0
    assert w_in.shape == (1, D_MODEL, IN_WIDTH) and meta_tokens.shape == (N_META, D_MODEL)
    d_ff = w_ffn_gate.shape[-1]

    w_in_bf = w_in[0].astype(BF16)
    g_mix = norm_mix_g[0].reshape(1, D)
    lb_raw = lb_raw.astype(F32)

    state0, hist0 = pl.pallas_call(
        _meta_state_kernel,
        out_shape=(jax.ShapeDtypeStruct((HEADS, HEAD_DIM, HEAD_DIM), F32),
                   jax.ShapeDtypeStruct((N_META, POOL_WIDTH), F32)),
        compiler_params=pltpu.CompilerParams(vmem_limit_bytes=VMEM_LIMIT_BYTES),
        name="meta_state",
    )(meta_tokens, g_mix,
      w_in_bf[:, OFF_F:OFF_F + HGRN_WIDTH], w_in_bf[:, OFF_I:OFF_I + HGRN_WIDTH],
      w_in_bf[:, OFF_POOL:OFF_POOL + POOL_WIDTH], lb_raw)

    T = MIXER_BLOCK
    h1 = pl.pallas_call(
        _mixer_kernel,
        grid=(S // T,),
        in_specs=[
            pl.BlockSpec((B, T, D), lambda t: (0, t, 0)),
            _resident((1, D)),
            _resident((D, IN_WIDTH)),
            _resident((2, HGRN_WIDTH)),
            _resident((1, HEAD_DIM)),
            _resident((len(POOL_WINDOWS), POOL_GROUP_DIM, POOL_GROUP_DIM)),
            _resident((1, POOL_WIDTH)),
            _resident((HGRN_WIDTH, D)),
            _resident((POOL_WIDTH, D)),
            _resident((D, D)),
            _resident((HEADS, HEAD_DIM, HEAD_DIM)),
            _resident((N_META, POOL_WIDTH)),
        ],
        out_specs=pl.BlockSpec((B, T, D), lambda t: (0, t, 0)),
        out_shape=jax.ShapeDtypeStruct((B, S, D), F32),
        scratch_shapes=[
            pltpu.VMEM((B * HEADS, HEAD_DIM, HEAD_DIM), F32),
            pltpu.VMEM((B, POOL_HISTORY, POOL_WIDTH), F32),
        ],
        compiler_params=pltpu.CompilerParams(
            dimension_semantics=("arbitrary",),
            vmem_limit_bytes=VMEM_LIMIT_BYTES),
        name="mixer",
    )(x, g_mix, w_in_bf, lb_raw, hgrn_norm_g[0].reshape(1, HEAD_DIM),
      pool_w[0].astype(BF16), pool_scale[0].reshape(1, POOL_WIDTH),
      w_branch_a[0].astype(BF16), w_branch_b[0].astype(BF16), w_out[0].astype(BF16),
      state0, hist0)

    TM = FFN_BLOCK
    out = pl.pallas_call(
        _ffn_kernel,
        grid=(B * S // TM,),
        in_specs=[
            pl.BlockSpec((TM, D), lambda i: (i, 0)),
            _resident((1, D)),
            _resident((D, d_ff)),
            _resident((D, d_ff)),
            _resident((d_ff, D)),
            _resident((1, D)),
        ],
        out_specs=pl.BlockSpec((TM, D), lambda i: (i, 0)),
        out_shape=jax.ShapeDtypeStruct((B * S, D), F32),
        compiler_params=pltpu.CompilerParams(
            dimension_semantics=("arbitrary",),
            vmem_limit_bytes=VMEM_LIMIT_BYTES),
        name="ffn",
    )(h1.reshape(B * S, D), norm_ffn_g[0].reshape(1, D),
      w_ffn_gate[0].astype(BF16), w_ffn_up[0].astype(BF16), w_ffn_down[0].astype(BF16),
      norm_final_g.reshape(1, D))
    return out.reshape(B, S, D)
```

```python
import jax
import jax.numpy as jnp
from jax import lax
from jax.experimental import pallas as pl
from jax.experimental.pallas import tpu as pltpu

D_MODEL = 1024
N_META = 16
HEADS = 8
HEAD_DIM = 128
HGRN_WIDTH = HEADS * HEAD_DIM
POOL_WINDOWS = (2, 4, 8, 16)
POOL_GROUP_DIM = 128
POOL_WIDTH = len(POOL_WINDOWS) * POOL_GROUP_DIM
POOL_HISTORY = 16
EPS = 1e-6

OFF_Q = 0
OFF_F = OFF_Q + HGRN_WIDTH
OFF_I = OFF_F + HGRN_WIDTH
OFF_OG = OFF_I + HGRN_WIDTH
OFF_POOL = OFF_OG + HGRN_WIDTH
OFF_GA = OFF_POOL + POOL_WIDTH
OFF_GB = OFF_GA + D_MODEL
IN_WIDTH = OFF_GB + D_MODEL

SUBLANES = 8
MIXER_BLOCK = 256
SCORE_CHUNK = 128
DENSE_COLS = 256
MID_LEVEL = 8
FFN_BLOCK = 1024
FFN_ROWS = 256
VMEM_LIMIT_BYTES = 56 * 1024 * 1024

F32 = jnp.float32
BF16 = jnp.bfloat16


def _rmsnorm(x, g):
    ms = jnp.mean(x * x, axis=-1, keepdims=True)
    return x * lax.rsqrt(ms + EPS) * g


def _sigmoid(x):
    return 0.5 * jnp.tanh(0.5 * x) + 0.5


def _silu(x):
    h = 0.5 * x
    return h * jnp.tanh(h) + h


def _dot(a, b):
    return jnp.dot(a, b, preferred_element_type=F32)


def _dot_nt(a, b):
    return lax.dot_general(a, b, (((1,), (1,)), ((), ())), preferred_element_type=F32)


def _dot_tn(a, b):
    return lax.dot_general(a, b, (((0,), (0,)), ((), ())), preferred_element_type=F32)


def _lower_bound(lb_raw):
    r0, r1 = lb_raw[0:1, :], lb_raw[1:2, :]
    m = jnp.maximum(r0, r1)
    e0, e1 = jnp.exp(r0 - m), jnp.exp(r1 - m)
    return e0 / (e0 + e1)


def _cumsum_rows(g, tri):
    g1 = g.astype(BF16)
    rem = g - g1.astype(F32)
    g2 = rem.astype(BF16)
    g3 = (rem - g2.astype(F32)).astype(BF16)
    return _dot(tri, g1) + _dot(tri, g2) + _dot(tri, g3)


def _tri(n):
    r = lax.broadcasted_iota(jnp.int32, (n, n), 0)
    c = lax.broadcasted_iota(jnp.int32, (n, n), 1)
    return (c <= r).astype(BF16)


def _forget_gate(zf, lb):
    f = lb + (1.0 - lb) * _sigmoid(zf)
    return 1.0 - f, f, jnp.log2(f)


def _meta_state_kernel(meta_ref, g_ref, wf_ref, wi_ref, wp_ref, lb_raw_ref,
                       state_ref, hist_ref):
    n = _rmsnorm(meta_ref[...], g_ref[...]).astype(BF16)
    lb = _lower_bound(lb_raw_ref[...])
    k, _, g = _forget_gate(_dot(n, wf_ref[...]), lb)
    v = _dot(n, wi_ref[...])
    hist_ref[...] = _dot(n, wp_ref[...])
    b = _cumsum_rows(g, _tri(N_META))
    k_end = (k * jnp.exp2(b[N_META - 1:N_META, :] - b)).astype(BF16)
    v = v.astype(BF16)
    for h in range(HEADS):
        sl = slice(h * HEAD_DIM, (h + 1) * HEAD_DIM)
        state_ref[h] = _dot_tn(v[:, sl], k_end[:, sl])


def _sibling_boundary(b, m, sub):
    T = b.shape[0]
    if m >= SUBLANES:
        nb = T // (2 * m)
        r = b.reshape(nb, 2 * m, HEAD_DIM)[:, m - 1:m, :]
        return jnp.broadcast_to(r, (nb, 2 * m, HEAD_DIM)).reshape(T, HEAD_DIM)
    groups = b.reshape(T // SUBLANES, SUBLANES, HEAD_DIM)

    def bcast(i):
        r = jnp.broadcast_to(groups[:, i:i + 1, :], groups.shape)
        return r.reshape(T, HEAD_DIM)

    r = bcast(m - 1)
    for first in range(2 * m, SUBLANES, 2 * m):
        r = jnp.where(sub >= first, bcast(first + m - 1), r)
    return r


def _level_constants(T):
    row = lax.broadcasted_iota(jnp.int32, (T, HEAD_DIM), 0)
    r_i = lax.broadcasted_iota(jnp.int32, (SCORE_CHUNK, SCORE_CHUNK), 0)
    c_i = lax.broadcasted_iota(jnp.int32, (SCORE_CHUNK, SCORE_CHUNK), 1)
    consts = {}
    m = 1
    while m < T:
        later = ((row // m) % 2) == 1
        sign = jnp.where(later, 1.0, -1.0)
        mask = None
        if m < SCORE_CHUNK:
            mask = (((r_i // (2 * m)) == (c_i // (2 * m)))
                    & ((r_i // m) % 2 == 1) & ((c_i // m) % 2 == 0))
        consts[m] = (later, sign, mask)
        m *= 2
    return row % SUBLANES, consts


def _hgrn_scores(q, k, f, b, state_t, sub, consts, tick):
    T = q.shape[0]
    n_chunks = T // SCORE_CHUNK
    o_prev = _dot_nt((q * jnp.exp2(b)).astype(BF16), state_t.astype(BF16))

    blocks = [[None] * n_chunks for _ in range(n_chunks)]
    m = 1
    while m < T:
        later, sign, mask = consts[m]
        if m == 1:
            z = jnp.where(later, q * f, k)
        else:
            r = _sibling_boundary(b, m, sub)
            z = jnp.where(later, q, k) * jnp.exp2((b - r) * sign)
        z = z.astype(BF16)
        zc = [z[c * SCORE_CHUNK:(c + 1) * SCORE_CHUNK] for c in range(n_chunks)]
        if m < SCORE_CHUNK:
            for c in range(n_chunks):
                s = _dot_nt(zc[c], zc[c])
                blocks[c][c] = jnp.where(mask, s, 0.0 if blocks[c][c] is None else blocks[c][c])
        else:
            span = m // SCORE_CHUNK
            for i in range(n_chunks):
                if (i // span) % 2 == 1:
                    for j in range((i // span - 1) * span, (i // span) * span):
                        blocks[i][j] = _dot_nt(zc[i], zc[j])
        if m == MID_LEVEL:
            tick()
        m *= 2
    return o_prev, blocks


def _hgrn_output(o_prev, blocks, q, k, v, b, state_t):
    T = q.shape[0]
    n_chunks = T // SCORE_CHUNK
    v_bf = v.astype(BF16)
    strips = []
    for i in range(n_chunks):
        a = jnp.concatenate(blocks[i][:i + 1], axis=1) if i else blocks[0][0]
        strips.append(_dot(a.astype(BF16), v_bf[:(i + 1) * SCORE_CHUNK]))
    o = o_prev + (jnp.concatenate(strips, axis=0) if n_chunks > 1 else strips[0])
    o = o + jnp.sum(q * k, axis=-1, keepdims=True) * v

    b_end = b[T - 1:T, :]
    k_end = (k * jnp.exp2(b_end - b)).astype(BF16)
    new_state_t = state_t * jnp.exp2(b_end) + _dot_tn(v_bf, k_end)
    return o, new_state_t


def _mixer_kernel(x_ref, g_mix_ref, w_in_ref, lb_raw_ref, g_hgrn_ref, pool_w_ref, pool_scale_ref,
                  w_a_ref, w_b_ref, w_out_ref, state0_ref, hist0_ref,
                  out_ref, state_ref, hist_ref):
    T = MIXER_BLOCK
    B = x_ref.shape[0]

    @pl.when(pl.program_id(0) == 0)
    def _():
        for b in range(B):
            for h in range(HEADS):
                state_ref[b * HEADS + h] = state0_ref[h]
            hist_ref[b] = hist0_ref[...]

    xs = [x_ref[b] for b in range(B)]
    ns = [_rmsnorm(x, g_mix_ref[...]).astype(BF16) for x in xs]
    lb = _lower_bound(lb_raw_ref[...])
    tri = _tri(T)
    sub, consts = _level_constants(T)
    g_hgrn = g_hgrn_ref[...]
    PAIR = 2 * HEAD_DIM

    queue = []

    def tick(n=1):
        for _ in range(n):
            if queue:
                queue.pop(0)[1]()

    def flush(tag=None):
        while queue if tag is None else any(t == tag for t, _ in queue):
            queue.pop(0)[1]()

    def enqueue_dot(tag, lhs, rhs_ref, col0, n_cols):
        blocks = [None] * (n_cols // DENSE_COLS)
        for cb in range(len(blocks)):
            def piece(cb=cb):
                cols = slice(col0 + cb * DENSE_COLS, col0 + (cb + 1) * DENSE_COLS)
                blocks[cb] = _dot(lhs(), rhs_ref[:, cols])
            queue.append((tag, piece))
        return blocks

    def assemble(blocks):
        return jnp.concatenate(blocks, axis=1) if len(blocks) > 1 else blocks[0]

    def enqueue_projection(tag, b, offset, width):
        return enqueue_dot(tag, lambda: ns[b], w_in_ref, offset, width)

    def enqueue_pair(b, p):
        return tuple(enqueue_projection(("pair", b, p), b, off + p * PAIR, PAIR)
                     for off in (OFF_Q, OFF_F, OFF_I, OFF_OG))

    def hgrn_pair(b, p, zq, zf, zi, zog):
        k, f, g = _forget_gate(zf, lb[:, p * PAIR:(p + 1) * PAIR])
        bsum = _cumsum_rows(g, tri)
        tick(2)
        q = _silu(zq)
        og = _silu(zog)
        heads = [(b * HEADS + 2 * p + j, slice(j * HEAD_DIM, (j + 1) * HEAD_DIM))
                 for j in range(2)]
        scores = []
        for h, sl in heads:
            scores.append(_hgrn_scores(q[:, sl], k[:, sl], f[:, sl], bsum[:, sl], state_ref[h],
                                       sub, consts, tick))
            tick()
        tick(2)
        ys = []
        for (h, sl), (o_prev, blocks) in zip(heads, scores):
            o, new_state = _hgrn_output(o_prev, blocks, q[:, sl], k[:, sl], zi[:, sl],
                                        bsum[:, sl], state_ref[h])
            state_ref[h] = new_state
            ys.append((_rmsnorm(o, g_hgrn) * og[:, sl]).astype(BF16))
        return jnp.concatenate(ys, axis=1)

    def pool_branch(b, u):
        ext = jnp.concatenate([hist_ref[b], u], axis=0)
        hist_ref[b] = u[T - POOL_HISTORY:, :]
        acc = ext
        span = 1
        ys = []
        for gi, w in enumerate(POOL_WINDOWS):
            while span < w:
                acc = acc + pltpu.roll(acc, span, 0)
                span *= 2
            sl = slice(gi * POOL_GROUP_DIM, (gi + 1) * POOL_GROUP_DIM)
            pooled = acc[POOL_HISTORY:, sl] * (1.0 / w) - u[:, sl]
            y = _dot(pooled.astype(BF16), pool_w_ref[gi]) * pool_scale_ref[:, sl]
            ys.append(y.astype(BF16))
        return jnp.concatenate(ys, axis=1)

    def enqueue_outputs(b, ya, z_pool, z_gb, z_ga):
        n_blocks = D_MODEL // DENSE_COLS
        hold = {}

        def start():
            hold["ya"] = jnp.concatenate(ya, axis=1)
            hold["yb"] = pool_branch(b, assemble(z_pool))
        queue.append((("out", b), start))
        branch_a = enqueue_dot(("out", b), lambda: hold["ya"], w_a_ref, 0, D_MODEL)
        branch_b = enqueue_dot(("out", b), lambda: hold["yb"], w_b_ref, 0, D_MODEL)

        def merge():
            merged = [_sigmoid(z_ga[cb]) * branch_a[cb]
                      + _sigmoid(z_gb[cb]) * branch_b[cb] for cb in range(n_blocks)]
            hold["merged"] = jnp.concatenate(merged, axis=1).astype(BF16)
        queue.append((("out", b), merge))
        for cb in range(n_blocks):
            def piece(cb=cb):
                cols = slice(cb * DENSE_COLS, (cb + 1) * DENSE_COLS)
                out_ref[b, :, cols] = xs[b][:, cols] + _dot(hold["merged"], w_out_ref[:, cols])
            queue.append((("out", b), piece))

    n_pairs = HEADS // 2
    z = {(0, 0): enqueue_pair(0, 0)}
    flush()
    others = {}
    for b in range(B):
        for p in range(1, n_pairs):
            z[(b, p)] = enqueue_pair(b, p)
        if b + 1 < B:
            z[(b + 1, 0)] = enqueue_pair(b + 1, 0)
        others[b] = tuple(enqueue_projection(("other", b), b, off, width) for off, width in
                          ((OFF_POOL, POOL_WIDTH), (OFF_GB, D_MODEL), (OFF_GA, D_MODEL)))
    for b in range(B):
        ya = []
        for p in range(n_pairs):
            flush(("pair", b, p))
            ya.append(hgrn_pair(b, p, *(assemble(blocks) for blocks in z[(b, p)])))
        enqueue_outputs(b, ya, *others[b])
    flush()


def _ffn_kernel(h_ref, g_ffn_ref, w_gate_ref, w_up_ref, w_down_ref, g_final_ref, out_ref):
    groups = [slice(r * FFN_ROWS, (r + 1) * FFN_ROWS) for r in range(FFN_BLOCK // FFN_ROWS)]

    def normed(rows):
        return _rmsnorm(h_ref[rows, :], g_ffn_ref[...]).astype(BF16)

    def finish(rows, down):
        out_ref[rows, :] = _rmsnorm(h_ref[rows, :] + down, g_final_ref[...])

    n = normed(groups[0])
    pending = None
    for i, rows in enumerate(groups):
        gate = _dot(n, w_gate_ref[...])
        up = _dot(n, w_up_ref[...])
        if pending is not None:
            finish(*pending)
        if i + 1 < len(groups):
            n = normed(groups[i + 1])
        act = (_silu(gate) * up).astype(BF16)
        pending = (rows, _dot(act, w_down_ref[...]))
    finish(*pending)


def _resident(shape):
    zeros = (0,) * len(shape)
    return pl.BlockSpec(shape, lambda *_: zeros, pipeline_mode=pl.Buffered(1))


def kernel(x, meta_tokens, norm_mix_g, w_in, lb_raw, hgrn_norm_g, pool_w, pool_scale,
           w_branch_a, w_branch_b, w_out, norm_ffn_g, w_ffn_gate, w_ffn_up, w_ffn_down,
           norm_final_g):
    B, S, D = x.shape
    assert D == D_MODEL and S % MIXER_BLOCK == 0 and (B * S) % FFN_BLOCK == 0
    assert w_in.shape == (1, D_MODEL, IN_WIDTH) and meta_tokens.shape == (N_META, D_MODEL)
    d_ff = w_ffn_gate.shape[-1]

    w_in_bf = w_in[0].astype(BF16)
    g_mix = norm_mix_g[0].reshape(1, D)
    lb_raw = lb_raw.astype(F32)

    state0, hist0 = pl.pallas_call(
        _meta_state_kernel,
        out_shape=(jax.ShapeDtypeStruct((HEADS, HEAD_DIM, HEAD_DIM), F32),
                   jax.ShapeDtypeStruct((N_META, POOL_WIDTH), F32)),
        compiler_params=pltpu.CompilerParams(vmem_limit_bytes=VMEM_LIMIT_BYTES),
        name="meta_state",
    )(meta_tokens, g_mix,
      w_in_bf[:, OFF_F:OFF_F + HGRN_WIDTH], w_in_bf[:, OFF_I:OFF_I + HGRN_WIDTH],
      w_in_bf[:, OFF_POOL:OFF_POOL + POOL_WIDTH], lb_raw)

    T = MIXER_BLOCK
    h1 = pl.pallas_call(
        _mixer_kernel,
        grid=(S // T,),
        in_specs=[
            pl.BlockSpec((B, T, D), lambda t: (0, t, 0)),
            _resident((1, D)),
            _resident((D, IN_WIDTH)),
            _resident((2, HGRN_WIDTH)),
            _resident((1, HEAD_DIM)),
            _resident((len(POOL_WINDOWS), POOL_GROUP_DIM, POOL_GROUP_DIM)),
            _resident((1, POOL_WIDTH)),
            _resident((HGRN_WIDTH, D)),
            _resident((POOL_WIDTH, D)),
            _resident((D, D)),
            _resident((HEADS, HEAD_DIM, HEAD_DIM)),
            _resident((N_META, POOL_WIDTH)),
        ],
        out_specs=pl.BlockSpec((B, T, D), lambda t: (0, t, 0)),
        out_shape=jax.ShapeDtypeStruct((B, S, D), F32),
        scratch_shapes=[
            pltpu.VMEM((B * HEADS, HEAD_DIM, HEAD_DIM), F32),
            pltpu.VMEM((B, POOL_HISTORY, POOL_WIDTH), F32),
        ],
        compiler_params=pltpu.CompilerParams(
            dimension_semantics=("arbitrary",),
            vmem_limit_bytes=VMEM_LIMIT_BYTES),
        name="mixer",
    )(x, g_mix, w_in_bf, lb_raw, hgrn_norm_g[0].reshape(1, HEAD_DIM),
      pool_w[0].astype(BF16), pool_scale[0].reshape(1, POOL_WIDTH),
      w_branch_a[0].astype(BF16), w_branch_b[0].astype(BF16), w_out[0].astype(BF16),
      state0, hist0)

    TM = FFN_BLOCK
    out = pl.pallas_call(
        _ffn_kernel,
        grid=(B * S // TM,),
        in_specs=[
            pl.BlockSpec((TM, D), lambda i: (i, 0)),
            _resident((1, D)),
            _resident((D, d_ff)),
            _resident((D, d_ff)),
            _resident((d_ff, D)),
            _resident((1, D)),
        ],
        out_specs=pl.BlockSpec((TM, D), lambda i: (i, 0)),
        out_shape=jax.ShapeDtypeStruct((B * S, D), F32),
        compiler_params=pltpu.CompilerParams(
            dimension_semantics=("arbitrary",),
            vmem_limit_bytes=VMEM_LIMIT_BYTES),
        name="ffn",
    )(h1.reshape(B * S, D), norm_ffn_g[0].reshape(1, D),
      w_ffn_gate[0].astype(BF16), w_ffn_up[0].astype(BF16), w_ffn_down[0].astype(BF16),
      norm_final_g.reshape(1, D))
    return out.reshape(B, S, D)
```

```python
import jax
import jax.numpy as jnp
from jax import lax
from jax.experimental import pallas as pl
from jax.experimental.pallas import tpu as pltpu

D_MODEL = 1024
N_META = 16
HEADS = 8
HEAD_DIM = 128
HGRN_WIDTH = HEADS * HEAD_DIM
POOL_WINDOWS = (2, 4, 8, 16)
POOL_GROUP_DIM = 128
POOL_WIDTH = len(POOL_WINDOWS) * POOL_GROUP_DIM
POOL_HISTORY = 16
EPS = 1e-6

OFF_Q = 0
OFF_F = OFF_Q + HGRN_WIDTH
OFF_I = OFF_F + HGRN_WIDTH
OFF_OG = OFF_I + HGRN_WIDTH
OFF_POOL = OFF_OG + HGRN_WIDTH
OFF_GA = OFF_POOL + POOL_WIDTH
OFF_GB = OFF_GA + D_MODEL
IN_WIDTH = OFF_GB + D_MODEL

SUBLANES = 8
BF16_ROWS = 2 * SUBLANES
MIXER_BLOCK = 256
SCORE_CHUNK = 128
DENSE_COLS = 256
MID_LEVEL = 8
FFN_BLOCK = 1024
FFN_ROWS = 256
VMEM_LIMIT_BYTES = 56 * 1024 * 1024

F32 = jnp.float32
BF16 = jnp.bfloat16


def _rmsnorm(x, g):
    ms = jnp.mean(x * x, axis=-1, keepdims=True)
    return x * lax.rsqrt(ms + EPS) * g


def _sigmoid(x):
    return 0.5 * jnp.tanh(0.5 * x) + 0.5


def _silu(x):
    h = 0.5 * x
    return h * jnp.tanh(h) + h


def _dot(a, b):
    return jnp.dot(a, b, preferred_element_type=F32)


def _dot_nt(a, b):
    return lax.dot_general(a, b, (((1,), (1,)), ((), ())), preferred_element_type=F32)


def _dot_tn(a, b):
    return lax.dot_general(a, b, (((0,), (0,)), ((), ())), preferred_element_type=F32)


def _lower_bound(lb_raw):
    r0, r1 = lb_raw[0:1, :], lb_raw[1:2, :]
    m = jnp.maximum(r0, r1)
    e0, e1 = jnp.exp(r0 - m), jnp.exp(r1 - m)
    return e0 / (e0 + e1)


def _cumsum_rows(g, tri):
    g1 = g.astype(BF16)
    rem = g - g1.astype(F32)
    g2 = rem.astype(BF16)
    g3 = (rem - g2.astype(F32)).astype(BF16)
    return _dot(tri, g1) + _dot(tri, g2) + _dot(tri, g3)


def _tri(n):
    r = lax.broadcasted_iota(jnp.int32, (n, n), 0)
    c = lax.broadcasted_iota(jnp.int32, (n, n), 1)
    return (c <= r).astype(BF16)


def _forget_gate(zf, lb):
    f = lb + (1.0 - lb) * _sigmoid(zf)
    return 1.0 - f, f, jnp.log2(f)


def _meta_state_kernel(meta_ref, g_ref, wf_ref, wi_ref, wp_ref, lb_raw_ref,
                       state_ref, hist_ref):
    n = _rmsnorm(meta_ref[...], g_ref[...]).astype(BF16)
    lb = _lower_bound(lb_raw_ref[...])
    k, _, g = _forget_gate(_dot(n, wf_ref[...]), lb)
    v = _dot(n, wi_ref[...])
    hist_ref[...] = _dot(n, wp_ref[...])
    b = _cumsum_rows(g, _tri(N_META))
    k_end = (k * jnp.exp2(b[N_META - 1:N_META, :] - b)).astype(BF16)
    v = v.astype(BF16)
    for h in range(HEADS):
        sl = slice(h * HEAD_DIM, (h + 1) * HEAD_DIM)
        state_ref[h] = _dot_tn(v[:, sl], k_end[:, sl])


def _sibling_boundary(b, m, sub):
    T = b.shape[0]
    if m >= SUBLANES:
        nb = T // (2 * m)
        r = b.reshape(nb, 2 * m, HEAD_DIM)[:, m - 1:m, :]
        return jnp.broadcast_to(r, (nb, 2 * m, HEAD_DIM)).reshape(T, HEAD_DIM)
    groups = b.reshape(T // SUBLANES, SUBLANES, HEAD_DIM)

    def bcast(i):
        r = jnp.broadcast_to(groups[:, i:i + 1, :], groups.shape)
        return r.reshape(T, HEAD_DIM)

    r = bcast(m - 1)
    for first in range(2 * m, SUBLANES, 2 * m):
        r = jnp.where(sub >= first, bcast(first + m - 1), r)
    return r


def _level_constants(T):
    row = lax.broadcasted_iota(jnp.int32, (T, HEAD_DIM), 0)
    r_i = lax.broadcasted_iota(jnp.int32, (SCORE_CHUNK, SCORE_CHUNK), 0)
    c_i = lax.broadcasted_iota(jnp.int32, (SCORE_CHUNK, SCORE_CHUNK), 1)
    consts = {}
    m = 1
    while m < T:
        later = ((row // m) % 2) == 1
        sign = jnp.where(later, 1.0, -1.0)
        mask = None
        if m < SCORE_CHUNK:
            mask = (((r_i // (2 * m)) == (c_i // (2 * m)))
                    & ((r_i // m) % 2 == 1) & ((c_i // m) % 2 == 0))
        consts[m] = (later, sign, mask)
        m *= 2
    return row % SUBLANES, consts


def _hgrn_scores(q, k, f, b, state_t, sub, consts, tick):
    T = q.shape[0]
    n_chunks = T // SCORE_CHUNK
    o_prev = _dot_nt((q * jnp.exp2(b)).astype(BF16), state_t.astype(BF16))

    blocks = [[None] * n_chunks for _ in range(n_chunks)]
    m = 2
    while m < T:
        later, sign, mask = consts[m]
        r = _sibling_boundary(b, m, sub)
        z = (jnp.where(later, q, k) * jnp.exp2((b - r) * sign)).astype(BF16)
        zc = [z[c * SCORE_CHUNK:(c + 1) * SCORE_CHUNK] for c in range(n_chunks)]
        if m < BF16_ROWS:
            for c in range(n_chunks):
                s = _dot_nt(zc[c], zc[c])
                blocks[c][c] = jnp.where(mask, s, 0.0 if blocks[c][c] is None else blocks[c][c])
        elif m < SCORE_CHUNK:
            pairs = SCORE_CHUNK // (2 * m)
            for c in range(n_chunks):
                queries = jnp.concatenate(
                    [zc[c][(2 * j + 1) * m:(2 * j + 2) * m] for j in range(pairs)], axis=0)
                s = _dot_nt(queries, zc[c])
                prev = blocks[c][c]
                rows = []
                for j in range(pairs):
                    lo, hi = (2 * j + 1) * m, (2 * j + 2) * m
                    rows.append(prev[lo - m:lo])
                    rows.append(jnp.where(mask[lo:hi], s[j * m:(j + 1) * m], prev[lo:hi]))
                blocks[c][c] = jnp.concatenate(rows, axis=0)
        else:
            span = m // SCORE_CHUNK
            for i in range(n_chunks):
                if (i // span) % 2 == 1:
                    for j in range((i // span - 1) * span, (i // span) * span):
                        blocks[i][j] = _dot_nt(zc[i], zc[j])
        if m == MID_LEVEL:
            tick()
        m *= 2
    return o_prev, blocks


def _previous_row(x):
    T = x.shape[0]
    groups = x.reshape(T // SUBLANES, SUBLANES, HEAD_DIM)
    return pltpu.roll(groups, 1, 1).reshape(T, HEAD_DIM)


def _hgrn_output(o_prev, blocks, q, k, f, v, b, state_t, odd):
    T = q.shape[0]
    n_chunks = T // SCORE_CHUNK
    v_bf = v.astype(BF16)
    strips = []
    for i in range(n_chunks):
        a = jnp.concatenate(blocks[i][:i + 1], axis=1) if i else blocks[0][0]
        strips.append(_dot(a.astype(BF16), v_bf[:(i + 1) * SCORE_CHUNK]))
    o = o_prev + (jnp.concatenate(strips, axis=0) if n_chunks > 1 else strips[0])
    o = o + jnp.sum(q * k, axis=-1, keepdims=True) * v
    pair = jnp.where(odd, q * f, 0.0) * _previous_row(k)
    o = o + jnp.sum(pair, axis=-1, keepdims=True) * _previous_row(v)

    b_end = b[T - 1:T, :]
    k_end = (k * jnp.exp2(b_end - b)).astype(BF16)
    new_state_t = state_t * jnp.exp2(b_end) + _dot_tn(v_bf, k_end)
    return o, new_state_t


def _mixer_kernel(x_ref, g_mix_ref, w_in_ref, lb_raw_ref, g_hgrn_ref, pool_w_ref, pool_scale_ref,
                  w_a_ref, w_b_ref, w_out_ref, state0_ref, hist0_ref,
                  out_ref, state_ref, hist_ref):
    T = MIXER_BLOCK
    B = x_ref.shape[0]

    @pl.when(pl.program_id(0) == 0)
    def _():
        for b in range(B):
            for h in range(HEADS):
                state_ref[b * HEADS + h] = state0_ref[h]
            hist_ref[b] = hist0_ref[...]

    xs = [x_ref[b] for b in range(B)]
    ns = [_rmsnorm(x, g_mix_ref[...]).astype(BF16) for x in xs]
    lb = _lower_bound(lb_raw_ref[...])
    tri = _tri(T)
    sub, consts = _level_constants(T)
    g_hgrn = g_hgrn_ref[...]
    PAIR = 2 * HEAD_DIM

    queue = []

    def tick(n=1):
        for _ in range(n):
            if queue:
                queue.pop(0)[1]()

    def flush(tag=None):
        while queue if tag is None else any(t == tag for t, _ in queue):
            queue.pop(0)[1]()

    def enqueue_dot(tag, lhs, rhs_ref, col0, n_cols):
        blocks = [None] * (n_cols // DENSE_COLS)
        for cb in range(len(blocks)):
            def piece(cb=cb):
                cols = slice(col0 + cb * DENSE_COLS, col0 + (cb + 1) * DENSE_COLS)
                blocks[cb] = _dot(lhs(), rhs_ref[:, cols])
            queue.append((tag, piece))
        return blocks

    def assemble(blocks):
        return jnp.concatenate(blocks, axis=1) if len(blocks) > 1 else blocks[0]

    def enqueue_projection(tag, b, offset, width):
        return enqueue_dot(tag, lambda: ns[b], w_in_ref, offset, width)

    def enqueue_pair(b, p):
        return tuple(enqueue_projection(("pair", b, p), b, off + p * PAIR, PAIR)
                     for off in (OFF_Q, OFF_F, OFF_I, OFF_OG))

    def hgrn_pair(b, p, zq, zf, zi, zog):
        k, f, g = _forget_gate(zf, lb[:, p * PAIR:(p + 1) * PAIR])
        bsum = _cumsum_rows(g, tri)
        tick(2)
        q = _silu(zq)
        og = _silu(zog)
        heads = [(b * HEADS + 2 * p + j, slice(j * HEAD_DIM, (j + 1) * HEAD_DIM))
                 for j in range(2)]
        scores = []
        for h, sl in heads:
            scores.append(_hgrn_scores(q[:, sl], k[:, sl], f[:, sl], bsum[:, sl], state_ref[h],
                                       sub, consts, tick))
            tick()
        tick(2)
        ys = []
        for (h, sl), (o_prev, blocks) in zip(heads, scores):
            o, new_state = _hgrn_output(o_prev, blocks, q[:, sl], k[:, sl], f[:, sl], zi[:, sl],
                                        bsum[:, sl], state_ref[h], consts[1][0])
            state_ref[h] = new_state
            ys.append((_rmsnorm(o, g_hgrn) * og[:, sl]).astype(BF16))
        return jnp.concatenate(ys, axis=1)

    def pool_branch(b, u):
        ext = jnp.concatenate([hist_ref[b], u], axis=0)
        hist_ref[b] = u[T - POOL_HISTORY:, :]
        acc = ext
        span = 1
        ys = []
        for gi, w in enumerate(POOL_WINDOWS):
            while span < w:
                acc = acc + pltpu.roll(acc, span, 0)
                span *= 2
            sl = slice(gi * POOL_GROUP_DIM, (gi + 1) * POOL_GROUP_DIM)
            pooled = acc[POOL_HISTORY:, sl] * (1.0 / w) - u[:, sl]
            y = _dot(pooled.astype(BF16), pool_w_ref[gi]) * pool_scale_ref[:, sl]
            ys.append(y.astype(BF16))
        return jnp.concatenate(ys, axis=1)

    def enqueue_outputs(b, ya, z_pool, z_gb, z_ga):
        n_blocks = D_MODEL // DENSE_COLS
        hold = {}

        def start():
            hold["ya"] = jnp.concatenate(ya, axis=1)
            hold["yb"] = pool_branch(b, assemble(z_pool))
        queue.append((("out", b), start))
        branch_a = enqueue_dot(("out", b), lambda: hold["ya"], w_a_ref, 0, D_MODEL)
        branch_b = enqueue_dot(("out", b), lambda: hold["yb"], w_b_ref, 0, D_MODEL)

        def merge():
            merged = [_sigmoid(z_ga[cb]) * branch_a[cb]
                      + _sigmoid(z_gb[cb]) * branch_b[cb] for cb in range(n_blocks)]
            hold["merged"] = jnp.concatenate(merged, axis=1).astype(BF16)
        queue.append((("out", b), merge))
        for cb in range(n_blocks):
            def piece(cb=cb):
                cols = slice(cb * DENSE_COLS, (cb + 1) * DENSE_COLS)
                out_ref[b, :, cols] = xs[b][:, cols] + _dot(hold["merged"], w_out_ref[:, cols])
            queue.append((("out", b), piece))

    n_pairs = HEADS // 2
    z = {(0, 0): enqueue_pair(0, 0)}
    flush()
    others = {}
    for b in range(B):
        for p in range(1, n_pairs):
            z[(b, p)] = enqueue_pair(b, p)
        if b + 1 < B:
            z[(b + 1, 0)] = enqueue_pair(b + 1, 0)
        others[b] = tuple(enqueue_projection(("other", b), b, off, width) for off, width in
                          ((OFF_POOL, POOL_WIDTH), (OFF_GB, D_MODEL), (OFF_GA, D_MODEL)))
    for b in range(B):
        ya = []
        for p in range(n_pairs):
            flush(("pair", b, p))
            ya.append(hgrn_pair(b, p, *(assemble(blocks) for blocks in z[(b, p)])))
        enqueue_outputs(b, ya, *others[b])
    flush()


def _ffn_kernel(h_ref, g_ffn_ref, w_gate_ref, w_up_ref, w_down_ref, g_final_ref, out_ref):
    groups = [slice(r * FFN_ROWS, (r + 1) * FFN_ROWS) for r in range(FFN_BLOCK // FFN_ROWS)]

    def normed(rows):
        return _rmsnorm(h_ref[rows, :], g_ffn_ref[...]).astype(BF16)

    def finish(rows, down):
        out_ref[rows, :] = _rmsnorm(h_ref[rows, :] + down, g_final_ref[...])

    n = normed(groups[0])
    pending = None
    for i, rows in enumerate(groups):
        gate = _dot(n, w_gate_ref[...])
        up = _dot(n, w_up_ref[...])
        if pending is not None:
            finish(*pending)
        if i + 1 < len(groups):
            n = normed(groups[i + 1])
        act = (_silu(gate) * up).astype(BF16)
        pending = (rows, _dot(act, w_down_ref[...]))
    finish(*pending)


def _resident(shape):
    zeros = (0,) * len(shape)
    return pl.BlockSpec(shape, lambda *_: zeros, pipeline_mode=pl.Buffered(1))


def kernel(x, meta_tokens, norm_mix_g, w_in, lb_raw, hgrn_norm_g, pool_w, pool_scale,
           w_branch_a, w_branch_b, w_out, norm_ffn_g, w_ffn_gate, w_ffn_up, w_ffn_down,
           norm_final_g):
    B, S, D = x.shape
    assert D == D_MODEL and S % MIXER_BLOCK == 0 and (B * S) % FFN_BLOCK == 0
    assert w_in.shape == (1, D_MODEL, IN_WIDTH) and meta_tokens.shape == (N_META, D_MODEL)
    d_ff = w_ffn_gate.shape[-1]

    w_in_bf = w_in[0].astype(BF16)
    g_mix = norm_mix_g[0].reshape(1, D)
    lb_raw = lb_raw.astype(F32)

    state0, hist0 = pl.pallas_call(
        _meta_state_kernel,
        out_shape=(jax.ShapeDtypeStruct((HEADS, HEAD_DIM, HEAD_DIM), F32),
                   jax.ShapeDtypeStruct((N_META, POOL_WIDTH), F32)),
        compiler_params=pltpu.CompilerParams(vmem_limit_bytes=VMEM_LIMIT_BYTES),
        name="meta_state",
    )(meta_tokens, g_mix,
      w_in_bf[:, OFF_F:OFF_F + HGRN_WIDTH], w_in_bf[:, OFF_I:OFF_I + HGRN_WIDTH],
      w_in_bf[:, OFF_POOL:OFF_POOL + POOL_WIDTH], lb_raw)

    T = MIXER_BLOCK
    h1 = pl.pallas_call(
        _mixer_kernel,
        grid=(S // T,),
        in_specs=[
            pl.BlockSpec((B, T, D), lambda t: (0, t, 0)),
            _resident((1, D)),
            _resident((D, IN_WIDTH)),
            _resident((2, HGRN_WIDTH)),
            _resident((1, HEAD_DIM)),
            _resident((len(POOL_WINDOWS), POOL_GROUP_DIM, POOL_GROUP_DIM)),
            _resident((1, POOL_WIDTH)),
            _resident((HGRN_WIDTH, D)),
            _resident((POOL_WIDTH, D)),
            _resident((D, D)),
            _resident((HEADS, HEAD_DIM, HEAD_DIM)),
            _resident((N_META, POOL_WIDTH)),
        ],
        out_specs=pl.BlockSpec((B, T, D), lambda t: (0, t, 0)),
        out_shape=jax.ShapeDtypeStruct((B, S, D), F32),
        scratch_shapes=[
            pltpu.VMEM((B * HEADS, HEAD_DIM, HEAD_DIM), F32),
            pltpu.VMEM((B, POOL_HISTORY, POOL_WIDTH), F32),
        ],
        compiler_params=pltpu.CompilerParams(
            dimension_semantics=("arbitrary",),
            vmem_limit_bytes=VMEM_LIMIT_BYTES),
        name="mixer",
    )(x, g_mix, w_in_bf, lb_raw, hgrn_norm_g[0].reshape(1, HEAD_DIM),
      pool_w[0].astype(BF16), pool_scale[0].reshape(1, POOL_WIDTH),
      w_branch_a[0].astype(BF16), w_branch_b[0].astype(BF16), w_out[0].astype(BF16),
      state0, hist0)

    TM = FFN_BLOCK
    out = pl.pallas_call(
        _ffn_kernel,
        grid=(B * S // TM,),
        in_specs=[
            pl.BlockSpec((TM, D), lambda i: (i, 0)),
            _resident((1, D)),
            _resident((D, d_ff)),
            _resident((D, d_ff)),
            _resident((d_ff, D)),
            _resident((1, D)),
        ],
        out_specs=pl.BlockSpec((TM, D), lambda i: (i, 0)),
        out_shape=jax.ShapeDtypeStruct((B * S, D), F32),
        compiler_params=pltpu.CompilerParams(
            dimension_semantics=("arbitrary",),
            vmem_limit_bytes=VMEM_LIMIT_BYTES),
        name="ffn",
    )(h1.reshape(B * S, D), norm_ffn_g[0].reshape(1, D),
      w_ffn_gate[0].astype(BF16), w_ffn_up[0].astype(BF16), w_ffn_down[0].astype(BF16),
      norm_final_g.reshape(1, D))
    return out.reshape(B, S, D)
```

```python
import jax
import jax.numpy as jnp
from jax import lax
from jax.experimental import pallas as pl
from jax.experimental.pallas import tpu as pltpu

D_MODEL = 1024
N_META = 16
HEADS = 8
HEAD_DIM = 128
HGRN_WIDTH = HEADS * HEAD_DIM
POOL_WINDOWS = (2, 4, 8, 16)
POOL_GROUP_DIM = 128
POOL_WIDTH = len(POOL_WINDOWS) * POOL_GROUP_DIM
POOL_HISTORY = 16
EPS = 1e-6

OFF_Q = 0
OFF_F = OFF_Q + HGRN_WIDTH
OFF_I = OFF_F + HGRN_WIDTH
OFF_OG = OFF_I + HGRN_WIDTH
OFF_POOL = OFF_OG + HGRN_WIDTH
OFF_GA = OFF_POOL + POOL_WIDTH
OFF_GB = OFF_GA + D_MODEL
IN_WIDTH = OFF_GB + D_MODEL

SUBLANES = 8
BF16_ROWS = 2 * SUBLANES
MIXER_BLOCK = 256
SCORE_CHUNK = 128
DENSE_COLS = 256
MID_LEVEL = 8
FFN_BLOCK = 1024
FFN_ROWS = 256
VMEM_LIMIT_BYTES = 56 * 1024 * 1024

F32 = jnp.float32
BF16 = jnp.bfloat16


def _rmsnorm(x, g):
    ms = jnp.mean(x * x, axis=-1, keepdims=True)
    return x * lax.rsqrt(ms + EPS) * g


def _sigmoid(x):
    return 0.5 * jnp.tanh(0.5 * x) + 0.5


def _silu(x):
    h = 0.5 * x
    return h * jnp.tanh(h) + h


def _dot(a, b):
    return jnp.dot(a, b, preferred_element_type=F32)


def _dot_nt(a, b):
    return lax.dot_general(a, b, (((1,), (1,)), ((), ())), preferred_element_type=F32)


def _dot_tn(a, b):
    return lax.dot_general(a, b, (((0,), (0,)), ((), ())), preferred_element_type=F32)


def _lower_bound(lb_raw):
    r0, r1 = lb_raw[0:1, :], lb_raw[1:2, :]
    m = jnp.maximum(r0, r1)
    e0, e1 = jnp.exp(r0 - m), jnp.exp(r1 - m)
    return e0 / (e0 + e1)


def _cumsum_rows(g, tri):
    g1 = g.astype(BF16)
    rem = g - g1.astype(F32)
    g2 = rem.astype(BF16)
    g3 = (rem - g2.astype(F32)).astype(BF16)
    return _dot(tri, g1) + _dot(tri, g2) + _dot(tri, g3)


def _tri(n):
    r = lax.broadcasted_iota(jnp.int32, (n, n), 0)
    c = lax.broadcasted_iota(jnp.int32, (n, n), 1)
    return (c <= r).astype(BF16)


def _forget_gate(zf, lb):
    f = lb + (1.0 - lb) * _sigmoid(zf)
    return 1.0 - f, f, jnp.log2(f)


def _meta_state_kernel(meta_ref, g_ref, wf_ref, wi_ref, wp_ref, lb_raw_ref,
                       state_ref, hist_ref):
    n = _rmsnorm(meta_ref[...], g_ref[...]).astype(BF16)
    lb = _lower_bound(lb_raw_ref[...])
    k, _, g = _forget_gate(_dot(n, wf_ref[...]), lb)
    v = _dot(n, wi_ref[...])
    hist_ref[...] = _dot(n, wp_ref[...])
    b = _cumsum_rows(g, _tri(N_META))
    k_end = (k * jnp.exp2(b[N_META - 1:N_META, :] - b)).astype(BF16)
    v = v.astype(BF16)
    for h in range(HEADS):
        sl = slice(h * HEAD_DIM, (h + 1) * HEAD_DIM)
        state_ref[h] = _dot_tn(v[:, sl], k_end[:, sl])


def _sibling_boundary(b, m, sub):
    T = b.shape[0]
    if m >= SUBLANES:
        nb = T // (2 * m)
        r = b.reshape(nb, 2 * m, HEAD_DIM)[:, m - 1:m, :]
        return jnp.broadcast_to(r, (nb, 2 * m, HEAD_DIM)).reshape(T, HEAD_DIM)
    groups = b.reshape(T // SUBLANES, SUBLANES, HEAD_DIM)

    def bcast(i):
        r = jnp.broadcast_to(groups[:, i:i + 1, :], groups.shape)
        return r.reshape(T, HEAD_DIM)

    r = bcast(m - 1)
    for first in range(2 * m, SUBLANES, 2 * m):
        r = jnp.where(sub >= first, bcast(first + m - 1), r)
    return r


def _level_constants(T):
    row = lax.broadcasted_iota(jnp.int32, (T, HEAD_DIM), 0)
    r_i = lax.broadcasted_iota(jnp.int32, (SCORE_CHUNK, SCORE_CHUNK), 0)
    c_i = lax.broadcasted_iota(jnp.int32, (SCORE_CHUNK, SCORE_CHUNK), 1)
    consts = {}
    m = 1
    while m < T:
        later = ((row // m) % 2) == 1
        sign = jnp.where(later, 1.0, -1.0)
        mask = None
        if m < SCORE_CHUNK:
            mask = (((r_i // (2 * m)) == (c_i // (2 * m)))
                    & ((r_i // m) % 2 == 1) & ((c_i // m) % 2 == 0))
        consts[m] = (later, sign, mask)
        m *= 2
    return row % SUBLANES, consts


def _hgrn_scores(q, k, f, b, state_t, sub, consts, tick):
    T = q.shape[0]
    n_chunks = T // SCORE_CHUNK
    o_prev = _dot_nt((q * jnp.exp2(b)).astype(BF16), state_t.astype(BF16))

    blocks = [[None] * n_chunks for _ in range(n_chunks)]
    m = 2
    while m < T:
        later, sign, mask = consts[m]
        r = _sibling_boundary(b, m, sub)
        z = (jnp.where(later, q, k) * jnp.exp2((b - r) * sign)).astype(BF16)
        zc = [z[c * SCORE_CHUNK:(c + 1) * SCORE_CHUNK] for c in range(n_chunks)]
        if m < BF16_ROWS:
            for c in range(n_chunks):
                s = _dot_nt(zc[c], zc[c])
                blocks[c][c] = jnp.where(mask, s, 0.0 if blocks[c][c] is None else blocks[c][c])
        elif m < SCORE_CHUNK:
            pairs = SCORE_CHUNK // (2 * m)
            for c in range(n_chunks):
                queries = jnp.concatenate(
                    [zc[c][(2 * j + 1) * m:(2 * j + 2) * m] for j in range(pairs)], axis=0)
                s = _dot_nt(queries, zc[c])
                prev = blocks[c][c]
                rows = []
                for j in range(pairs):
                    lo, hi = (2 * j + 1) * m, (2 * j + 2) * m
                    rows.append(prev[lo - m:lo])
                    rows.append(jnp.where(mask[lo:hi], s[j * m:(j + 1) * m], prev[lo:hi]))
                blocks[c][c] = jnp.concatenate(rows, axis=0)
        else:
            span = m // SCORE_CHUNK
            for i in range(n_chunks):
                if (i // span) % 2 == 1:
                    for j in range((i // span - 1) * span, (i // span) * span):
                        blocks[i][j] = _dot_nt(zc[i], zc[j])
        if m == MID_LEVEL:
            tick()
        m *= 2
    return o_prev, blocks


def _previous_row(x):
    T = x.shape[0]
    groups = x.reshape(T // SUBLANES, SUBLANES, HEAD_DIM)
    return pltpu.roll(groups, 1, 1).reshape(T, HEAD_DIM)


def _hgrn_output(o_prev, blocks, q, k, f, v, b, state_t, odd):
    T = q.shape[0]
    n_chunks = T // SCORE_CHUNK
    v_bf = v.astype(BF16)
    strips = []
    for i in range(n_chunks):
        a = jnp.concatenate(blocks[i][:i + 1], axis=1) if i else blocks[0][0]
        strips.append(_dot(a.astype(BF16), v_bf[:(i + 1) * SCORE_CHUNK]))
    o = o_prev + (jnp.concatenate(strips, axis=0) if n_chunks > 1 else strips[0])
    o = o + jnp.sum(q * k, axis=-1, keepdims=True) * v
    pair = jnp.where(odd, q * f, 0.0) * _previous_row(k)
    o = o + jnp.sum(pair, axis=-1, keepdims=True) * _previous_row(v)

    b_end = b[T - 1:T, :]
    k_end = (k * jnp.exp2(b_end - b)).astype(BF16)
    new_state_t = state_t * jnp.exp2(b_end) + _dot_tn(v_bf, k_end)
    return o, new_state_t


def _mixer_kernel(x_ref, g_mix_ref, w_in_ref, lb_raw_ref, g_hgrn_ref, pool_w_ref, pool_scale_ref,
                  w_a_ref, w_b_ref, w_out_ref, state0_ref, hist0_ref,
                  ffn_gate_ref, ffn_up_ref, ffn_down_ref,
                  out_ref, ffn_gate_bf_ref, ffn_up_bf_ref, ffn_down_bf_ref,
                  state_ref, hist_ref):
    T = MIXER_BLOCK
    B = x_ref.shape[0]

    @pl.when(pl.program_id(0) == 0)
    def _():
        for b in range(B):
            for h in range(HEADS):
                state_ref[b * HEADS + h] = state0_ref[h]
            hist_ref[b] = hist0_ref[...]

    ffn_gate_bf_ref[...] = ffn_gate_ref[...].astype(BF16)
    ffn_up_bf_ref[...] = ffn_up_ref[...].astype(BF16)
    ffn_down_bf_ref[...] = ffn_down_ref[...].astype(BF16)

    xs = [x_ref[b] for b in range(B)]
    ns = [_rmsnorm(x, g_mix_ref[...]).astype(BF16) for x in xs]
    lb = _lower_bound(lb_raw_ref[...])
    tri = _tri(T)
    sub, consts = _level_constants(T)
    g_hgrn = g_hgrn_ref[...]
    PAIR = 2 * HEAD_DIM

    queue = []

    def tick(n=1):
        for _ in range(n):
            if queue:
                queue.pop(0)[1]()

    def flush(tag=None):
        while queue if tag is None else any(t == tag for t, _ in queue):
            queue.pop(0)[1]()

    def enqueue_dot(tag, lhs, rhs_ref, col0, n_cols):
        blocks = [None] * (n_cols // DENSE_COLS)
        for cb in range(len(blocks)):
            def piece(cb=cb):
                cols = slice(col0 + cb * DENSE_COLS, col0 + (cb + 1) * DENSE_COLS)
                blocks[cb] = _dot(lhs(), rhs_ref[:, cols])
            queue.append((tag, piece))
        return blocks

    def assemble(blocks):
        return jnp.concatenate(blocks, axis=1) if len(blocks) > 1 else blocks[0]

    def enqueue_projection(tag, b, offset, width):
        return enqueue_dot(tag, lambda: ns[b], w_in_ref, offset, width)

    def enqueue_pair(b, p):
        return tuple(enqueue_projection(("pair", b, p), b, off + p * PAIR, PAIR)
                     for off in (OFF_Q, OFF_F, OFF_I, OFF_OG))

    def hgrn_pair(b, p, zq, zf, zi, zog):
        k, f, g = _forget_gate(zf, lb[:, p * PAIR:(p + 1) * PAIR])
        bsum = _cumsum_rows(g, tri)
        tick(2)
        q = _silu(zq)
        og = _silu(zog)
        heads = [(b * HEADS + 2 * p + j, slice(j * HEAD_DIM, (j + 1) * HEAD_DIM))
                 for j in range(2)]
        scores = []
        for h, sl in heads:
            scores.append(_hgrn_scores(q[:, sl], k[:, sl], f[:, sl], bsum[:, sl], state_ref[h],
                                       sub, consts, tick))
            tick()
        tick(2)
        ys = []
        for (h, sl), (o_prev, blocks) in zip(heads, scores):
            o, new_state = _hgrn_output(o_prev, blocks, q[:, sl], k[:, sl], f[:, sl], zi[:, sl],
                                        bsum[:, sl], state_ref[h], consts[1][0])
            state_ref[h] = new_state
            ys.append((_rmsnorm(o, g_hgrn) * og[:, sl]).astype(BF16))
        return jnp.concatenate(ys, axis=1)

    def pool_branch(b, u):
        ext = jnp.concatenate([hist_ref[b], u], axis=0)
        hist_ref[b] = u[T - POOL_HISTORY:, :]
        acc = ext
        span = 1
        ys = []
        for gi, w in enumerate(POOL_WINDOWS):
            while span < w:
                acc = acc + pltpu.roll(acc, span, 0)
                span *= 2
            sl = slice(gi * POOL_GROUP_DIM, (gi + 1) * POOL_GROUP_DIM)
            pooled = acc[POOL_HISTORY:, sl] * (1.0 / w) - u[:, sl]
            y = _dot(pooled.astype(BF16), pool_w_ref[gi]) * pool_scale_ref[:, sl]
            ys.append(y.astype(BF16))
        return jnp.concatenate(ys, axis=1)

    def enqueue_outputs(b, ya, z_pool, z_gb, z_ga):
        n_blocks = D_MODEL // DENSE_COLS
        hold = {}

        def start():
            hold["ya"] = jnp.concatenate(ya, axis=1)
            hold["yb"] = pool_branch(b, assemble(z_pool))
        queue.append((("out", b), start))
        branch_a = enqueue_dot(("out", b), lambda: hold["ya"], w_a_ref, 0, D_MODEL)
        branch_b = enqueue_dot(("out", b), lambda: hold["yb"], w_b_ref, 0, D_MODEL)

        def merge():
            merged = [_sigmoid(z_ga[cb]) * branch_a[cb]
                      + _sigmoid(z_gb[cb]) * branch_b[cb] for cb in range(n_blocks)]
            hold["merged"] = jnp.concatenate(merged, axis=1).astype(BF16)
        queue.append((("out", b), merge))
        for cb in range(n_blocks):
            def piece(cb=cb):
                cols = slice(cb * DENSE_COLS, (cb + 1) * DENSE_COLS)
                out_ref[b, :, cols] = xs[b][:, cols] + _dot(hold["merged"], w_out_ref[:, cols])
            queue.append((("out", b), piece))

    n_pairs = HEADS // 2
    z = {(0, 0): enqueue_pair(0, 0)}
    flush()
    others = {}
    for b in range(B):
        for p in range(1, n_pairs):
            z[(b, p)] = enqueue_pair(b, p)
        if b + 1 < B:
            z[(b + 1, 0)] = enqueue_pair(b + 1, 0)
        others[b] = tuple(enqueue_projection(("other", b), b, off, width) for off, width in
                          ((OFF_POOL, POOL_WIDTH), (OFF_GB, D_MODEL), (OFF_GA, D_MODEL)))
    for b in range(B):
        ya = []
        for p in range(n_pairs):
            flush(("pair", b, p))
            ya.append(hgrn_pair(b, p, *(assemble(blocks) for blocks in z[(b, p)])))
        enqueue_outputs(b, ya, *others[b])
    flush()


def _ffn_kernel(h_ref, g_ffn_ref, w_gate_ref, w_up_ref, w_down_ref, g_final_ref, out_ref):
    groups = [slice(r * FFN_ROWS, (r + 1) * FFN_ROWS) for r in range(FFN_BLOCK // FFN_ROWS)]

    def normed(rows):
        return _rmsnorm(h_ref[rows, :], g_ffn_ref[...]).astype(BF16)

    def finish(rows, down):
        out_ref[rows, :] = _rmsnorm(h_ref[rows, :] + down, g_final_ref[...])

    n = normed(groups[0])
    pending = None
    for i, rows in enumerate(groups):
        gate = _dot(n, w_gate_ref[...])
        up = _dot(n, w_up_ref[...])
        if pending is not None:
            finish(*pending)
        if i + 1 < len(groups):
            n = normed(groups[i + 1])
        act = (_silu(gate) * up).astype(BF16)
        pending = (rows, _dot(act, w_down_ref[...]))
    finish(*pending)


def _resident(shape):
    zeros = (0,) * len(shape)
    return pl.BlockSpec(shape, lambda *_: zeros, pipeline_mode=pl.Buffered(1))


def _row_slabs(rows, cols, steps):
    slab = next(r for r in range(BF16_ROWS, rows + 1, BF16_ROWS)
                if rows % r == 0 and rows // r <= steps)
    last = rows // slab - 1
    return pl.BlockSpec((slab, cols), lambda t: (jnp.minimum(t, last), 0))


def kernel(x, meta_tokens, norm_mix_g, w_in, lb_raw, hgrn_norm_g, pool_w, pool_scale,
           w_branch_a, w_branch_b, w_out, norm_ffn_g, w_ffn_gate, w_ffn_up, w_ffn_down,
           norm_final_g):
    B, S, D = x.shape
    assert D == D_MODEL and S % MIXER_BLOCK == 0 and (B * S) % FFN_BLOCK == 0
    assert w_in.shape == (1, D_MODEL, IN_WIDTH) and meta_tokens.shape == (N_META, D_MODEL)
    d_ff = w_ffn_gate.shape[-1]

    w_in_bf = w_in[0].astype(BF16)
    g_mix = norm_mix_g[0].reshape(1, D)
    lb_raw = lb_raw.astype(F32)

    state0, hist0 = pl.pallas_call(
        _meta_state_kernel,
        out_shape=(jax.ShapeDtypeStruct((HEADS, HEAD_DIM, HEAD_DIM), F32),
                   jax.ShapeDtypeStruct((N_META, POOL_WIDTH), F32)),
        compiler_params=pltpu.CompilerParams(vmem_limit_bytes=VMEM_LIMIT_BYTES),
        name="meta_state",
    )(meta_tokens, g_mix,
      w_in_bf[:, OFF_F:OFF_F + HGRN_WIDTH], w_in_bf[:, OFF_I:OFF_I + HGRN_WIDTH],
      w_in_bf[:, OFF_POOL:OFF_POOL + POOL_WIDTH], lb_raw)

    T = MIXER_BLOCK
    steps = S // T
    ffn_weight_specs = [_row_slabs(D, d_ff, steps), _row_slabs(D, d_ff, steps),
                        _row_slabs(d_ff, D, steps)]
    h1, w_gate_bf, w_up_bf, w_down_bf = pl.pallas_call(
        _mixer_kernel,
        grid=(steps,),
        in_specs=[
            pl.BlockSpec((B, T, D), lambda t: (0, t, 0)),
            _resident((1, D)),
            _resident((D, IN_WIDTH)),
            _resident((2, HGRN_WIDTH)),
            _resident((1, HEAD_DIM)),
            _resident((len(POOL_WINDOWS), POOL_GROUP_DIM, POOL_GROUP_DIM)),
            _resident((1, POOL_WIDTH)),
            _resident((HGRN_WIDTH, D)),
            _resident((POOL_WIDTH, D)),
            _resident((D, D)),
            _resident((HEADS, HEAD_DIM, HEAD_DIM)),
            _resident((N_META, POOL_WIDTH)),
        ] + ffn_weight_specs,
        out_specs=[pl.BlockSpec((B, T, D), lambda t: (0, t, 0))] + ffn_weight_specs,
        out_shape=(jax.ShapeDtypeStruct((B, S, D), F32),
                   jax.ShapeDtypeStruct((D, d_ff), BF16),
                   jax.ShapeDtypeStruct((D, d_ff), BF16),
                   jax.ShapeDtypeStruct((d_ff, D), BF16)),
        scratch_shapes=[
            pltpu.VMEM((B * HEADS, HEAD_DIM, HEAD_DIM), F32),
            pltpu.VMEM((B, POOL_HISTORY, POOL_WIDTH), F32),
        ],
        compiler_params=pltpu.CompilerParams(
            dimension_semantics=("arbitrary",),
            vmem_limit_bytes=VMEM_LIMIT_BYTES),
        name="mixer",
    )(x, g_mix, w_in_bf, lb_raw, hgrn_norm_g[0].reshape(1, HEAD_DIM),
      pool_w[0].astype(BF16), pool_scale[0].reshape(1, POOL_WIDTH),
      w_branch_a[0].astype(BF16), w_branch_b[0].astype(BF16), w_out[0].astype(BF16),
      state0, hist0, w_ffn_gate[0], w_ffn_up[0], w_ffn_down[0])

    TM = FFN_BLOCK
    out = pl.pallas_call(
        _ffn_kernel,
        grid=(B * S // TM,),
        in_specs=[
            pl.BlockSpec((TM, D), lambda i: (i, 0)),
            _resident((1, D)),
            _resident((D, d_ff)),
            _resident((D, d_ff)),
            _resident((d_ff, D)),
            _resident((1, D)),
        ],
        out_specs=pl.BlockSpec((TM, D), lambda i: (i, 0)),
        out_shape=jax.ShapeDtypeStruct((B * S, D), F32),
        compiler_params=pltpu.CompilerParams(
            dimension_semantics=("arbitrary",),
            vmem_limit_bytes=VMEM_LIMIT_BYTES),
        name="ffn",
    )(h1.reshape(B * S, D), norm_ffn_g[0].reshape(1, D), w_gate_bf, w_up_bf, w_down_bf,
      norm_final_g.reshape(1, D))
    return out.reshape(B, S, D)
```

```python
import jax
import jax.numpy as jnp
from jax import lax
from jax.experimental import pallas as pl
from jax.experimental.pallas import tpu as pltpu

D_MODEL = 1024
N_META = 16
HEADS = 8
HEAD_DIM = 128
HGRN_WIDTH = HEADS * HEAD_DIM
POOL_WINDOWS = (2, 4, 8, 16)
POOL_GROUP_DIM = 128
POOL_WIDTH = len(POOL_WINDOWS) * POOL_GROUP_DIM
POOL_HISTORY = 16
EPS = 1e-6

OFF_Q = 0
OFF_F = OFF_Q + HGRN_WIDTH
OFF_I = OFF_F + HGRN_WIDTH
OFF_OG = OFF_I + HGRN_WIDTH
OFF_POOL = OFF_OG + HGRN_WIDTH
OFF_GA = OFF_POOL + POOL_WIDTH
OFF_GB = OFF_GA + D_MODEL
IN_WIDTH = OFF_GB + D_MODEL

SUBLANES = 8
BF16_ROWS = 2 * SUBLANES
MIXER_BLOCK = 256
SCORE_CHUNK = 128
DENSE_COLS = 256
MID_LEVEL = 8
FFN_BLOCK = 1024
FFN_ROWS = 256
VMEM_LIMIT_BYTES = 56 * 1024 * 1024

F32 = jnp.float32
BF16 = jnp.bfloat16


def _rmsnorm(x, g):
    ms = jnp.mean(x * x, axis=-1, keepdims=True)
    return x * lax.rsqrt(ms + EPS) * g


def _sigmoid(x):
    return 0.5 * jnp.tanh(0.5 * x) + 0.5


def _silu(x):
    h = 0.5 * x
    return h * jnp.tanh(h) + h


def _dot(a, b):
    return jnp.dot(a, b, preferred_element_type=F32)


def _dot_nt(a, b):
    return lax.dot_general(a, b, (((1,), (1,)), ((), ())), preferred_element_type=F32)


def _dot_tn(a, b):
    return lax.dot_general(a, b, (((0,), (0,)), ((), ())), preferred_element_type=F32)


def _lower_bound(lb_raw):
    r0, r1 = lb_raw[0:1, :], lb_raw[1:2, :]
    m = jnp.maximum(r0, r1)
    e0, e1 = jnp.exp(r0 - m), jnp.exp(r1 - m)
    return e0 / (e0 + e1)


def _cumsum_rows(g, tri):
    g1 = g.astype(BF16)
    rem = g - g1.astype(F32)
    g2 = rem.astype(BF16)
    g3 = (rem - g2.astype(F32)).astype(BF16)
    return _dot(tri, g1) + _dot(tri, g2) + _dot(tri, g3)


def _tri(n):
    r = lax.broadcasted_iota(jnp.int32, (n, n), 0)
    c = lax.broadcasted_iota(jnp.int32, (n, n), 1)
    return (c <= r).astype(BF16)


def _forget_gate(zf, lb):
    f = lb + (1.0 - lb) * _sigmoid(zf)
    return 1.0 - f, f, jnp.log2(f)


def _meta_state_kernel(meta_ref, g_ref, wf_ref, wi_ref, wp_ref, lb_raw_ref,
                       state_ref, hist_ref):
    n = _rmsnorm(meta_ref[...], g_ref[...]).astype(BF16)
    lb = _lower_bound(lb_raw_ref[...])
    k, _, g = _forget_gate(_dot(n, wf_ref[...]), lb)
    v = _dot(n, wi_ref[...])
    hist_ref[...] = _dot(n, wp_ref[...])
    b = _cumsum_rows(g, _tri(N_META))
    k_end = (k * jnp.exp2(b[N_META - 1:N_META, :] - b)).astype(BF16)
    v = v.astype(BF16)
    for h in range(HEADS):
        sl = slice(h * HEAD_DIM, (h + 1) * HEAD_DIM)
        state_ref[h] = _dot_tn(v[:, sl], k_end[:, sl])


def _sibling_boundary(b, m, sub):
    T = b.shape[0]
    if m >= SUBLANES:
        nb = T // (2 * m)
        r = b.reshape(nb, 2 * m, HEAD_DIM)[:, m - 1:m, :]
        return jnp.broadcast_to(r, (nb, 2 * m, HEAD_DIM)).reshape(T, HEAD_DIM)
    groups = b.reshape(T // SUBLANES, SUBLANES, HEAD_DIM)

    def bcast(i):
        r = jnp.broadcast_to(groups[:, i:i + 1, :], groups.shape)
        return r.reshape(T, HEAD_DIM)

    r = bcast(m - 1)
    for first in range(2 * m, SUBLANES, 2 * m):
        r = jnp.where(sub >= first, bcast(first + m - 1), r)
    return r


def _level_constants(T):
    row = lax.broadcasted_iota(jnp.int32, (T, HEAD_DIM), 0)
    r_i = lax.broadcasted_iota(jnp.int32, (SCORE_CHUNK, SCORE_CHUNK), 0)
    c_i = lax.broadcasted_iota(jnp.int32, (SCORE_CHUNK, SCORE_CHUNK), 1)
    consts = {}
    m = 1
    while m < T:
        later = ((row // m) % 2) == 1
        sign = jnp.where(later, 1.0, -1.0)
        mask = None
        if m < SCORE_CHUNK:
            mask = (((r_i // (2 * m)) == (c_i // (2 * m)))
                    & ((r_i // m) % 2 == 1) & ((c_i // m) % 2 == 0))
        consts[m] = (later, sign, mask)
        m *= 2
    return row % SUBLANES, consts


def _hgrn_scores(q, k, f, b, state_t, sub, consts, tick):
    T = q.shape[0]
    n_chunks = T // SCORE_CHUNK
    o_prev = _dot_nt((q * jnp.exp2(b)).astype(BF16), state_t.astype(BF16))

    blocks = [[None] * n_chunks for _ in range(n_chunks)]
    m = 2
    while m < T:
        later, sign, mask = consts[m]
        r = _sibling_boundary(b, m, sub)
        z = (jnp.where(later, q, k) * jnp.exp2((b - r) * sign)).astype(BF16)
        zc = [z[c * SCORE_CHUNK:(c + 1) * SCORE_CHUNK] for c in range(n_chunks)]
        if m < BF16_ROWS:
            for c in range(n_chunks):
                s = _dot_nt(zc[c], zc[c])
                blocks[c][c] = jnp.where(mask, s, 0.0 if blocks[c][c] is None else blocks[c][c])
        elif m < SCORE_CHUNK:
            pairs = SCORE_CHUNK // (2 * m)
            for c in range(n_chunks):
                queries = jnp.concatenate(
                    [zc[c][(2 * j + 1) * m:(2 * j + 2) * m] for j in range(pairs)], axis=0)
                s = _dot_nt(queries, zc[c])
                prev = blocks[c][c]
                rows = []
                for j in range(pairs):
                    lo, hi = (2 * j + 1) * m, (2 * j + 2) * m
                    rows.append(prev[lo - m:lo])
                    rows.append(jnp.where(mask[lo:hi], s[j * m:(j + 1) * m], prev[lo:hi]))
                blocks[c][c] = jnp.concatenate(rows, axis=0)
        else:
            span = m // SCORE_CHUNK
            for i in range(n_chunks):
                if (i // span) % 2 == 1:
                    for j in range((i // span - 1) * span, (i // span) * span):
                        blocks[i][j] = _dot_nt(zc[i], zc[j])
        if m == MID_LEVEL:
            tick()
        m *= 2
    return o_prev, blocks


def _previous_row(x):
    T = x.shape[0]
    groups = x.reshape(T // SUBLANES, SUBLANES, HEAD_DIM)
    return pltpu.roll(groups, 1, 1).reshape(T, HEAD_DIM)


def _hgrn_output(o_prev, blocks, q, k, f, v, b, state_t, odd):
    T = q.shape[0]
    n_chunks = T // SCORE_CHUNK
    v_bf = v.astype(BF16)
    strips = []
    for i in range(n_chunks):
        a = jnp.concatenate(blocks[i][:i + 1], axis=1) if i else blocks[0][0]
        strips.append(_dot(a.astype(BF16), v_bf[:(i + 1) * SCORE_CHUNK]))
    o = o_prev + (jnp.concatenate(strips, axis=0) if n_chunks > 1 else strips[0])
    o = o + jnp.sum(q * k, axis=-1, keepdims=True) * v
    pair = jnp.where(odd, q * f, 0.0) * _previous_row(k)
    o = o + jnp.sum(pair, axis=-1, keepdims=True) * _previous_row(v)

    b_end = b[T - 1:T, :]
    k_end = (k * jnp.exp2(b_end - b)).astype(BF16)
    new_state_t = state_t * jnp.exp2(b_end) + _dot_tn(v_bf, k_end)
    return o, new_state_t


def _mixer_kernel(x_ref, g_mix_ref, w_in_ref, lb_raw_ref, g_hgrn_ref, pool_w_ref, pool_scale_ref,
                  w_a_ref, w_b_ref, w_out_ref, state0_ref, hist0_ref,
                  ffn_gate_ref, ffn_up_ref, ffn_down_ref,
                  out_ref, ffn_gate_bf_ref, ffn_up_bf_ref, ffn_down_bf_ref,
                  state_ref, hist_ref):
    T = MIXER_BLOCK
    B = x_ref.shape[0]

    @pl.when(pl.program_id(0) == 0)
    def _():
        for b in range(B):
            for h in range(HEADS):
                state_ref[b * HEADS + h] = state0_ref[h]
            hist_ref[b] = hist0_ref[...]

    ffn_gate_bf_ref[...] = ffn_gate_ref[...].astype(BF16)
    ffn_up_bf_ref[...] = ffn_up_ref[...].astype(BF16)
    ffn_down_bf_ref[...] = ffn_down_ref[...].astype(BF16)

    xs = [x_ref[b] for b in range(B)]
    ns = [_rmsnorm(x, g_mix_ref[...]).astype(BF16) for x in xs]
    lb = _lower_bound(lb_raw_ref[...])
    tri = _tri(T)
    sub, consts = _level_constants(T)
    g_hgrn = g_hgrn_ref[...]
    PAIR = 2 * HEAD_DIM

    queue = []

    def tick(n=1):
        for _ in range(n):
            if queue:
                queue.pop(0)[1]()

    def flush(tag=None):
        while queue if tag is None else any(t == tag for t, _ in queue):
            queue.pop(0)[1]()

    def enqueue_dot(tag, lhs, rhs_ref, col0, n_cols):
        blocks = [None] * (n_cols // DENSE_COLS)
        for cb in range(len(blocks)):
            def piece(cb=cb):
                cols = slice(col0 + cb * DENSE_COLS, col0 + (cb + 1) * DENSE_COLS)
                blocks[cb] = _dot(lhs(), rhs_ref[:, cols])
            queue.append((tag, piece))
        return blocks

    def assemble(blocks):
        return jnp.concatenate(blocks, axis=1) if len(blocks) > 1 else blocks[0]

    def enqueue_projection(tag, b, offset, width):
        return enqueue_dot(tag, lambda: ns[b], w_in_ref, offset, width)

    def enqueue_pair(b, p):
        return tuple(enqueue_projection(("pair", b, p), b, off + p * PAIR, PAIR)
                     for off in (OFF_Q, OFF_F, OFF_I, OFF_OG))

    def hgrn_pair(b, p, zq, zf, zi, zog):
        k, f, g = _forget_gate(zf, lb[:, p * PAIR:(p + 1) * PAIR])
        bsum = _cumsum_rows(g, tri)
        tick(2)
        q = _silu(zq)
        og = _silu(zog)
        heads = [(b * HEADS + 2 * p + j, slice(j * HEAD_DIM, (j + 1) * HEAD_DIM))
                 for j in range(2)]
        scores = []
        for h, sl in heads:
            scores.append(_hgrn_scores(q[:, sl], k[:, sl], f[:, sl], bsum[:, sl], state_ref[h],
                                       sub, consts, tick))
            tick()
        tick(2)
        ys = []
        for (h, sl), (o_prev, blocks) in zip(heads, scores):
            o, new_state = _hgrn_output(o_prev, blocks, q[:, sl], k[:, sl], f[:, sl], zi[:, sl],
                                        bsum[:, sl], state_ref[h], consts[1][0])
            state_ref[h] = new_state
            ys.append((_rmsnorm(o, g_hgrn) * og[:, sl]).astype(BF16))
        return jnp.concatenate(ys, axis=1)

    def pool_branch(b, u):
        ext = jnp.concatenate([hist_ref[b], u], axis=0)
        hist_ref[b] = u[T - POOL_HISTORY:, :]
        acc = ext
        span = 1
        ys = []
        for gi, w in enumerate(POOL_WINDOWS):
            while span < w:
                acc = acc + pltpu.roll(acc, span, 0)
                span *= 2
            sl = slice(gi * POOL_GROUP_DIM, (gi + 1) * POOL_GROUP_DIM)
            pooled = acc[POOL_HISTORY:, sl] * (1.0 / w) - u[:, sl]
            y = _dot(pooled.astype(BF16), pool_w_ref[gi]) * pool_scale_ref[:, sl]
            ys.append(y.astype(BF16))
        return jnp.concatenate(ys, axis=1)

    def enqueue_outputs(b, ya, z_pool, z_gb, z_ga):
        n_blocks = D_MODEL // DENSE_COLS
        hold = {}

        def start():
            hold["ya"] = jnp.concatenate(ya, axis=1)
            hold["yb"] = pool_branch(b, assemble(z_pool))
        queue.append((("out", b), start))
        branch_a = enqueue_dot(("out", b), lambda: hold["ya"], w_a_ref, 0, D_MODEL)
        branch_b = enqueue_dot(("out", b), lambda: hold["yb"], w_b_ref, 0, D_MODEL)

        def merge():
            merged = [_sigmoid(z_ga[cb]) * branch_a[cb]
                      + _sigmoid(z_gb[cb]) * branch_b[cb] for cb in range(n_blocks)]
            hold["merged"] = jnp.concatenate(merged, axis=1).astype(BF16)
        queue.append((("out", b), merge))
        for cb in range(n_blocks):
            def piece(cb=cb):
                cols = slice(cb * DENSE_COLS, (cb + 1) * DENSE_COLS)
                out_ref[b, :, cols] = xs[b][:, cols] + _dot(hold["merged"], w_out_ref[:, cols])
            queue.append((("out", b), piece))

    n_pairs = HEADS // 2
    z = {(0, 0): enqueue_pair(0, 0)}
    flush()
    others = {}
    for b in range(B):
        for p in range(1, n_pairs):
            z[(b, p)] = enqueue_pair(b, p)
        if b + 1 < B:
            z[(b + 1, 0)] = enqueue_pair(b + 1, 0)
        others[b] = tuple(enqueue_projection(("other", b), b, off, width) for off, width in
                          ((OFF_POOL, POOL_WIDTH), (OFF_GB, D_MODEL), (OFF_GA, D_MODEL)))
    for b in range(B):
        ya = []
        for p in range(n_pairs):
            flush(("pair", b, p))
            ya.append(hgrn_pair(b, p, *(assemble(blocks) for blocks in z[(b, p)])))
        enqueue_outputs(b, ya, *others[b])
    flush()


def _ffn_kernel(h_ref, g_ffn_ref, w_gate_ref, w_up_ref, w_down_ref, g_final_ref, out_ref):
    groups = [slice(r * FFN_ROWS, (r + 1) * FFN_ROWS) for r in range(FFN_BLOCK // FFN_ROWS)]

    def normed(rows):
        return _rmsnorm(h_ref[rows, :], g_ffn_ref[...]).astype(BF16)

    def finish(rows, down):
        out_ref[rows, :] = _rmsnorm(h_ref[rows, :] + down, g_final_ref[...])

    def activation(n):
        d_ff = w_gate_ref.shape[1]
        acts = []
        for c0 in range(0, d_ff, DENSE_COLS):
            cols = slice(c0, c0 + DENSE_COLS)
            gate = _dot(n, w_gate_ref[:, cols])
            up = _dot(n, w_up_ref[:, cols])
            acts.append((_silu(gate) * up).astype(BF16))
        return jnp.concatenate(acts, axis=1)

    n = normed(groups[0])
    pending = None
    for i, rows in enumerate(groups):
        act = activation(n)
        if pending is not None:
            finish(*pending)
        if i + 1 < len(groups):
            n = normed(groups[i + 1])
        pending = (rows, _dot(act, w_down_ref[...]))
    finish(*pending)


def _resident(shape):
    zeros = (0,) * len(shape)
    return pl.BlockSpec(shape, lambda *_: zeros, pipeline_mode=pl.Buffered(1))


def _row_slabs(rows, cols, steps):
    slab = next(r for r in range(BF16_ROWS, rows + 1, BF16_ROWS)
                if rows % r == 0 and rows // r <= steps)
    last = rows // slab - 1
    return pl.BlockSpec((slab, cols), lambda t: (jnp.minimum(t, last), 0))


def kernel(x, meta_tokens, norm_mix_g, w_in, lb_raw, hgrn_norm_g, pool_w, pool_scale,
           w_branch_a, w_branch_b, w_out, norm_ffn_g, w_ffn_gate, w_ffn_up, w_ffn_down,
           norm_final_g):
    B, S, D = x.shape
    assert D == D_MODEL and S % MIXER_BLOCK == 0 and (B * S) % FFN_BLOCK == 0
    assert w_in.shape == (1, D_MODEL, IN_WIDTH) and meta_tokens.shape == (N_META, D_MODEL)
    d_ff = w_ffn_gate.shape[-1]

    w_in_bf = w_in[0].astype(BF16)
    g_mix = norm_mix_g[0].reshape(1, D)
    lb_raw = lb_raw.astype(F32)

    def whole(shape):
        zeros = (0,) * len(shape)
        return pl.BlockSpec(shape, lambda i: zeros)

    def w_in_columns(offset, width):
        assert offset % width == 0
        return pl.BlockSpec((D, width), lambda i: (0, offset // width))

    state0, hist0 = pl.pallas_call(
        _meta_state_kernel,
        grid=(1,),
        in_specs=[whole((N_META, D)), whole((1, D)),
                  w_in_columns(OFF_F, HGRN_WIDTH), w_in_columns(OFF_I, HGRN_WIDTH),
                  w_in_columns(OFF_POOL, POOL_WIDTH), whole((2, HGRN_WIDTH))],
        out_specs=(whole((HEADS, HEAD_DIM, HEAD_DIM)), whole((N_META, POOL_WIDTH))),
        out_shape=(jax.ShapeDtypeStruct((HEADS, HEAD_DIM, HEAD_DIM), F32),
                   jax.ShapeDtypeStruct((N_META, POOL_WIDTH), F32)),
        compiler_params=pltpu.CompilerParams(vmem_limit_bytes=VMEM_LIMIT_BYTES),
        name="meta_state",
    )(meta_tokens, g_mix, w_in_bf, w_in_bf, w_in_bf, lb_raw)

    T = MIXER_BLOCK
    steps = S // T
    ffn_weight_specs = [_row_slabs(D, d_ff, steps), _row_slabs(D, d_ff, steps),
                        _row_slabs(d_ff, D, steps)]
    h1, w_gate_bf, w_up_bf, w_down_bf = pl.pallas_call(
        _mixer_kernel,
        grid=(steps,),
        in_specs=[
            pl.BlockSpec((B, T, D), lambda t: (0, t, 0)),
            _resident((1, D)),
            _resident((D, IN_WIDTH)),
            _resident((2, HGRN_WIDTH)),
            _resident((1, HEAD_DIM)),
            _resident((len(POOL_WINDOWS), POOL_GROUP_DIM, POOL_GROUP_DIM)),
            _resident((1, POOL_WIDTH)),
            _resident((HGRN_WIDTH, D)),
            _resident((POOL_WIDTH, D)),
            _resident((D, D)),
            _resident((HEADS, HEAD_DIM, HEAD_DIM)),
            _resident((N_META, POOL_WIDTH)),
        ] + ffn_weight_specs,
        out_specs=[pl.BlockSpec((B, T, D), lambda t: (0, t, 0))] + ffn_weight_specs,
        out_shape=(jax.ShapeDtypeStruct((B, S, D), F32),
                   jax.ShapeDtypeStruct((D, d_ff), BF16),
                   jax.ShapeDtypeStruct((D, d_ff), BF16),
                   jax.ShapeDtypeStruct((d_ff, D), BF16)),
        scratch_shapes=[
            pltpu.VMEM((B * HEADS, HEAD_DIM, HEAD_DIM), F32),
            pltpu.VMEM((B, POOL_HISTORY, POOL_WIDTH), F32),
        ],
        compiler_params=pltpu.CompilerParams(
            dimension_semantics=("arbitrary",),
            vmem_limit_bytes=VMEM_LIMIT_BYTES),
        name="mixer",
    )(x, g_mix, w_in_bf, lb_raw, hgrn_norm_g[0].reshape(1, HEAD_DIM),
      pool_w[0].astype(BF16), pool_scale[0].reshape(1, POOL_WIDTH),
      w_branch_a[0].astype(BF16), w_branch_b[0].astype(BF16), w_out[0].astype(BF16),
      state0, hist0, w_ffn_gate[0], w_ffn_up[0], w_ffn_down[0])

    TM = FFN_BLOCK
    out = pl.pallas_call(
        _ffn_kernel,
        grid=(B * S // TM,),
        in_specs=[
            pl.BlockSpec((TM, D), lambda i: (i, 0)),
            _resident((1, D)),
            _resident((D, d_ff)),
            _resident((D, d_ff)),
            _resident((d_ff, D)),
            _resident((1, D)),
        ],
        out_specs=pl.BlockSpec((TM, D), lambda i: (i, 0)),
        out_shape=jax.ShapeDtypeStruct((B * S, D), F32),
        compiler_params=pltpu.CompilerParams(
            dimension_semantics=("arbitrary",),
            vmem_limit_bytes=VMEM_LIMIT_BYTES),
        name="ffn",
    )(h1.reshape(B * S, D), norm_ffn_g[0].reshape(1, D), w_gate_bf, w_up_bf, w_down_bf,
      norm_final_g.reshape(1, D))
    return out.reshape(B, S, D)
```

```python
import jax
import jax.numpy as jnp
from jax import lax
from jax.experimental import pallas as pl
from jax.experimental.pallas import tpu as pltpu

D_MODEL = 1024
N_META = 16
HEADS = 8
HEAD_DIM = 128
HGRN_WIDTH = HEADS * HEAD_DIM
POOL_WINDOWS = (2, 4, 8, 16)
POOL_GROUP_DIM = 128
POOL_WIDTH = len(POOL_WINDOWS) * POOL_GROUP_DIM
POOL_HISTORY = 16
EPS = 1e-6

OFF_Q = 0
OFF_F = OFF_Q + HGRN_WIDTH
OFF_I = OFF_F + HGRN_WIDTH
OFF_OG = OFF_I + HGRN_WIDTH
OFF_POOL = OFF_OG + HGRN_WIDTH
OFF_GA = OFF_POOL + POOL_WIDTH
OFF_GB = OFF_GA + D_MODEL
IN_WIDTH = OFF_GB + D_MODEL

SUBLANES = 8
BF16_ROWS = 2 * SUBLANES
MIXER_BLOCK = 256
SCORE_CHUNK = 128
DENSE_COLS = 256
MID_LEVEL = 8
FFN_BLOCK = 2048
FFN_ROWS = 256
VMEM_LIMIT_BYTES = 56 * 1024 * 1024

F32 = jnp.float32
BF16 = jnp.bfloat16


def _rmsnorm(x, g):
    ms = jnp.mean(x * x, axis=-1, keepdims=True)
    return x * lax.rsqrt(ms + EPS) * g


def _sigmoid(x):
    return 0.5 * jnp.tanh(0.5 * x) + 0.5


def _silu(x):
    h = 0.5 * x
    return h * jnp.tanh(h) + h


def _dot(a, b):
    return jnp.dot(a, b, preferred_element_type=F32)


def _dot_nt(a, b):
    return lax.dot_general(a, b, (((1,), (1,)), ((), ())), preferred_element_type=F32)


def _dot_tn(a, b):
    return lax.dot_general(a, b, (((0,), (0,)), ((), ())), preferred_element_type=F32)


def _lower_bound(lb_raw):
    r0, r1 = lb_raw[0:1, :], lb_raw[1:2, :]
    m = jnp.maximum(r0, r1)
    e0, e1 = jnp.exp(r0 - m), jnp.exp(r1 - m)
    return e0 / (e0 + e1)


def _cumsum_rows(g, tri):
    g1 = g.astype(BF16)
    rem = g - g1.astype(F32)
    g2 = rem.astype(BF16)
    g3 = (rem - g2.astype(F32)).astype(BF16)
    return _dot(tri, g1) + _dot(tri, g2) + _dot(tri, g3)


def _tri(n):
    r = lax.broadcasted_iota(jnp.int32, (n, n), 0)
    c = lax.broadcasted_iota(jnp.int32, (n, n), 1)
    return (c <= r).astype(BF16)


def _forget_gate(zf, lb):
    f = lb + (1.0 - lb) * _sigmoid(zf)
    return 1.0 - f, f, jnp.log2(f)


def _meta_state_kernel(meta_ref, g_ref, wf_ref, wi_ref, wp_ref, lb_raw_ref,
                       state_ref, hist_ref):
    n = _rmsnorm(meta_ref[...], g_ref[...]).astype(BF16)
    lb = _lower_bound(lb_raw_ref[...])
    k, _, g = _forget_gate(_dot(n, wf_ref[...]), lb)
    v = _dot(n, wi_ref[...])
    hist_ref[...] = _dot(n, wp_ref[...])
    b = _cumsum_rows(g, _tri(N_META))
    k_end = (k * jnp.exp2(b[N_META - 1:N_META, :] - b)).astype(BF16)
    v = v.astype(BF16)
    for h in range(HEADS):
        sl = slice(h * HEAD_DIM, (h + 1) * HEAD_DIM)
        state_ref[h] = _dot_tn(v[:, sl], k_end[:, sl])


def _sibling_boundary(b, m, sub):
    T = b.shape[0]
    if m >= SUBLANES:
        nb = T // (2 * m)
        r = b.reshape(nb, 2 * m, HEAD_DIM)[:, m - 1:m, :]
        return jnp.broadcast_to(r, (nb, 2 * m, HEAD_DIM)).reshape(T, HEAD_DIM)
    groups = b.reshape(T // SUBLANES, SUBLANES, HEAD_DIM)

    def bcast(i):
        r = jnp.broadcast_to(groups[:, i:i + 1, :], groups.shape)
        return r.reshape(T, HEAD_DIM)

    r = bcast(m - 1)
    for first in range(2 * m, SUBLANES, 2 * m):
        r = jnp.where(sub >= first, bcast(first + m - 1), r)
    return r


def _level_constants(T):
    row = lax.broadcasted_iota(jnp.int32, (T, HEAD_DIM), 0)
    r_i = lax.broadcasted_iota(jnp.int32, (SCORE_CHUNK, SCORE_CHUNK), 0)
    c_i = lax.broadcasted_iota(jnp.int32, (SCORE_CHUNK, SCORE_CHUNK), 1)
    consts = {}
    m = 1
    while m < T:
        later = ((row // m) % 2) == 1
        sign = jnp.where(later, 1.0, -1.0)
        mask = None
        if m < SCORE_CHUNK:
            mask = (((r_i // (2 * m)) == (c_i // (2 * m)))
                    & ((r_i // m) % 2 == 1) & ((c_i // m) % 2 == 0))
        consts[m] = (later, sign, mask)
        m *= 2
    return row % SUBLANES, consts


def _hgrn_scores(q, k, f, b, state_t, sub, consts, tick):
    T = q.shape[0]
    n_chunks = T // SCORE_CHUNK
    o_prev = _dot_nt((q * jnp.exp2(b)).astype(BF16), state_t.astype(BF16))

    blocks = [[None] * n_chunks for _ in range(n_chunks)]
    m = 2
    while m < T:
        later, sign, mask = consts[m]
        r = _sibling_boundary(b, m, sub)
        z = (jnp.where(later, q, k) * jnp.exp2((b - r) * sign)).astype(BF16)
        zc = [z[c * SCORE_CHUNK:(c + 1) * SCORE_CHUNK] for c in range(n_chunks)]
        if m < BF16_ROWS:
            for c in range(n_chunks):
                s = _dot_nt(zc[c], zc[c])
                blocks[c][c] = jnp.where(mask, s, 0.0 if blocks[c][c] is None else blocks[c][c])
        elif m < SCORE_CHUNK:
            pairs = SCORE_CHUNK // (2 * m)
            for c in range(n_chunks):
                queries = jnp.concatenate(
                    [zc[c][(2 * j + 1) * m:(2 * j + 2) * m] for j in range(pairs)], axis=0)
                s = _dot_nt(queries, zc[c])
                prev = blocks[c][c]
                rows = []
                for j in range(pairs):
                    lo, hi = (2 * j + 1) * m, (2 * j + 2) * m
                    rows.append(prev[lo - m:lo])
                    rows.append(jnp.where(mask[lo:hi], s[j * m:(j + 1) * m], prev[lo:hi]))
                blocks[c][c] = jnp.concatenate(rows, axis=0)
        else:
            span = m // SCORE_CHUNK
            for i in range(n_chunks):
                if (i // span) % 2 == 1:
                    for j in range((i // span - 1) * span, (i // span) * span):
                        blocks[i][j] = _dot_nt(zc[i], zc[j])
        if m == MID_LEVEL:
            tick()
        m *= 2
    return o_prev, blocks


def _previous_row(x):
    T = x.shape[0]
    groups = x.reshape(T // SUBLANES, SUBLANES, HEAD_DIM)
    return pltpu.roll(groups, 1, 1).reshape(T, HEAD_DIM)


def _hgrn_output(o_prev, blocks, q, k, f, v, b, state_t, odd):
    T = q.shape[0]
    n_chunks = T // SCORE_CHUNK
    v_bf = v.astype(BF16)
    strips = []
    for i in range(n_chunks):
        a = jnp.concatenate(blocks[i][:i + 1], axis=1) if i else blocks[0][0]
        strips.append(_dot(a.astype(BF16), v_bf[:(i + 1) * SCORE_CHUNK]))
    o = o_prev + (jnp.concatenate(strips, axis=0) if n_chunks > 1 else strips[0])
    o = o + jnp.sum(q * k, axis=-1, keepdims=True) * v
    pair = jnp.where(odd, q * f, 0.0) * _previous_row(k)
    o = o + jnp.sum(pair, axis=-1, keepdims=True) * _previous_row(v)

    b_end = b[T - 1:T, :]
    k_end = (k * jnp.exp2(b_end - b)).astype(BF16)
    new_state_t = state_t * jnp.exp2(b_end) + _dot_tn(v_bf, k_end)
    return o, new_state_t


def _mixer_kernel(x_ref, g_mix_ref, w_in_ref, lb_raw_ref, g_hgrn_ref, pool_w_ref, pool_scale_ref,
                  w_a_ref, w_b_ref, w_out_ref, state0_ref, hist0_ref,
                  ffn_gate_ref, ffn_up_ref, ffn_down_ref,
                  out_ref, ffn_gate_bf_ref, ffn_up_bf_ref, ffn_down_bf_ref,
                  state_ref, hist_ref):
    T = MIXER_BLOCK
    B = x_ref.shape[0]

    @pl.when(pl.program_id(0) == 0)
    def _():
        for b in range(B):
            for h in range(HEADS):
                state_ref[b * HEADS + h] = state0_ref[h]
            hist_ref[b] = hist0_ref[...]

    ffn_gate_bf_ref[...] = ffn_gate_ref[...].astype(BF16)
    ffn_up_bf_ref[...] = ffn_up_ref[...].astype(BF16)
    ffn_down_bf_ref[...] = ffn_down_ref[...].astype(BF16)

    xs = [x_ref[b] for b in range(B)]
    ns = [_rmsnorm(x, g_mix_ref[...]).astype(BF16) for x in xs]
    lb = _lower_bound(lb_raw_ref[...])
    tri = _tri(T)
    sub, consts = _level_constants(T)
    g_hgrn = g_hgrn_ref[...]
    PAIR = 2 * HEAD_DIM

    queue = []

    def tick(n=1):
        for _ in range(n):
            if queue:
                queue.pop(0)[1]()

    def flush(tag=None):
        while queue if tag is None else any(t == tag for t, _ in queue):
            queue.pop(0)[1]()

    def enqueue_dot(tag, lhs, rhs_ref, col0, n_cols):
        blocks = [None] * (n_cols // DENSE_COLS)
        for cb in range(len(blocks)):
            def piece(cb=cb):
                cols = slice(col0 + cb * DENSE_COLS, col0 + (cb + 1) * DENSE_COLS)
                blocks[cb] = _dot(lhs(), rhs_ref[:, cols])
            queue.append((tag, piece))
        return blocks

    def assemble(blocks):
        return jnp.concatenate(blocks, axis=1) if len(blocks) > 1 else blocks[0]

    def enqueue_projection(tag, b, offset, width):
        return enqueue_dot(tag, lambda: ns[b], w_in_ref, offset, width)

    def enqueue_pair(b, p):
        return tuple(enqueue_projection(("pair", b, p), b, off + p * PAIR, PAIR)
                     for off in (OFF_Q, OFF_F, OFF_I, OFF_OG))

    def hgrn_pair(b, p, zq, zf, zi, zog):
        k, f, g = _forget_gate(zf, lb[:, p * PAIR:(p + 1) * PAIR])
        bsum = _cumsum_rows(g, tri)
        tick(2)
        q = _silu(zq)
        og = _silu(zog)
        heads = [(b * HEADS + 2 * p + j, slice(j * HEAD_DIM, (j + 1) * HEAD_DIM))
                 for j in range(2)]
        scores = []
        for h, sl in heads:
            scores.append(_hgrn_scores(q[:, sl], k[:, sl], f[:, sl], bsum[:, sl], state_ref[h],
                                       sub, consts, tick))
            tick()
        tick(2)
        ys = []
        for (h, sl), (o_prev, blocks) in zip(heads, scores):
            o, new_state = _hgrn_output(o_prev, blocks, q[:, sl], k[:, sl], f[:, sl], zi[:, sl],
                                        bsum[:, sl], state_ref[h], consts[1][0])
            state_ref[h] = new_state
            ys.append((_rmsnorm(o, g_hgrn) * og[:, sl]).astype(BF16))
        return jnp.concatenate(ys, axis=1)

    def pool_branch(b, u):
        ext = jnp.concatenate([hist_ref[b], u], axis=0)
        hist_ref[b] = u[T - POOL_HISTORY:, :]
        acc = ext
        span = 1
        ys = []
        for gi, w in enumerate(POOL_WINDOWS):
            while span < w:
                acc = acc + pltpu.roll(acc, span, 0)
                span *= 2
            sl = slice(gi * POOL_GROUP_DIM, (gi + 1) * POOL_GROUP_DIM)
            pooled = acc[POOL_HISTORY:, sl] * (1.0 / w) - u[:, sl]
            y = _dot(pooled.astype(BF16), pool_w_ref[gi]) * pool_scale_ref[:, sl]
            ys.append(y.astype(BF16))
        return jnp.concatenate(ys, axis=1)

    def enqueue_outputs(b, ya, z_pool, z_gb, z_ga):
        n_blocks = D_MODEL // DENSE_COLS
        hold = {}

        def start():
            hold["ya"] = jnp.concatenate(ya, axis=1)
            hold["yb"] = pool_branch(b, assemble(z_pool))
        queue.append((("out", b), start))
        branch_a = enqueue_dot(("out", b), lambda: hold["ya"], w_a_ref, 0, D_MODEL)
        branch_b = enqueue_dot(("out", b), lambda: hold["yb"], w_b_ref, 0, D_MODEL)

        def merge():
            merged = [_sigmoid(z_ga[cb]) * branch_a[cb]
                      + _sigmoid(z_gb[cb]) * branch_b[cb] for cb in range(n_blocks)]
            hold["merged"] = jnp.concatenate(merged, axis=1).astype(BF16)
        queue.append((("out", b), merge))
        for cb in range(n_blocks):
            def piece(cb=cb):
                cols = slice(cb * DENSE_COLS, (cb + 1) * DENSE_COLS)
                out_ref[b, :, cols] = xs[b][:, cols] + _dot(hold["merged"], w_out_ref[:, cols])
            queue.append((("out", b), piece))

    n_pairs = HEADS // 2
    z = {(0, 0): enqueue_pair(0, 0)}
    flush()
    others = {}
    for b in range(B):
        for p in range(1, n_pairs):
            z[(b, p)] = enqueue_pair(b, p)
        if b + 1 < B:
            z[(b + 1, 0)] = enqueue_pair(b + 1, 0)
        others[b] = tuple(enqueue_projection(("other", b), b, off, width) for off, width in
                          ((OFF_POOL, POOL_WIDTH), (OFF_GB, D_MODEL), (OFF_GA, D_MODEL)))
    for b in range(B):
        ya = []
        for p in range(n_pairs):
            flush(("pair", b, p))
            ya.append(hgrn_pair(b, p, *(assemble(blocks) for blocks in z[(b, p)])))
        enqueue_outputs(b, ya, *others[b])
    flush()


def _ffn_kernel(h_ref, g_ffn_ref, w_gate_ref, w_up_ref, w_down_ref, g_final_ref, out_ref):
    groups = [slice(r * FFN_ROWS, (r + 1) * FFN_ROWS) for r in range(FFN_BLOCK // FFN_ROWS)]

    def normed(rows):
        return _rmsnorm(h_ref[rows, :], g_ffn_ref[...]).astype(BF16)

    def finish(rows, down):
        out_ref[rows, :] = _rmsnorm(h_ref[rows, :] + down, g_final_ref[...])

    def activation(n):
        d_ff = w_gate_ref.shape[1]
        acts = []
        for c0 in range(0, d_ff, DENSE_COLS):
            cols = slice(c0, c0 + DENSE_COLS)
            gate = _dot(n, w_gate_ref[:, cols])
            up = _dot(n, w_up_ref[:, cols])
            acts.append((_silu(gate) * up).astype(BF16))
        return jnp.concatenate(acts, axis=1)

    n = normed(groups[0])
    pending = None
    for i, rows in enumerate(groups):
        act = activation(n)
        if pending is not None:
            finish(*pending)
        if i + 1 < len(groups):
            n = normed(groups[i + 1])
        pending = (rows, _dot(act, w_down_ref[...]))
    finish(*pending)


def _resident(shape):
    zeros = (0,) * len(shape)
    return pl.BlockSpec(shape, lambda *_: zeros, pipeline_mode=pl.Buffered(1))


def _row_slabs(rows, cols, steps):
    slab = next(r for r in range(BF16_ROWS, rows + 1, BF16_ROWS)
                if rows % r == 0 and rows // r <= steps)
    last = rows // slab - 1
    return pl.BlockSpec((slab, cols), lambda t: (jnp.minimum(t, last), 0))


def kernel(x, meta_tokens, norm_mix_g, w_in, lb_raw, hgrn_norm_g, pool_w, pool_scale,
           w_branch_a, w_branch_b, w_out, norm_ffn_g, w_ffn_gate, w_ffn_up, w_ffn_down,
           norm_final_g):
    B, S, D = x.shape
    assert D == D_MODEL and S % MIXER_BLOCK == 0 and (B * S) % FFN_BLOCK == 0
    assert w_in.shape == (1, D_MODEL, IN_WIDTH) and meta_tokens.shape == (N_META, D_MODEL)
    d_ff = w_ffn_gate.shape[-1]

    w_in_bf = w_in[0].astype(BF16)
    g_mix = norm_mix_g[0].reshape(1, D)
    lb_raw = lb_raw.astype(F32)

    def whole(shape):
        zeros = (0,) * len(shape)
        return pl.BlockSpec(shape, lambda i: zeros)

    def w_in_columns(offset, width):
        assert offset % width == 0
        return pl.BlockSpec((D, width), lambda i: (0, offset // width))

    state0, hist0 = pl.pallas_call(
        _meta_state_kernel,
        grid=(1,),
        in_specs=[whole((N_META, D)), whole((1, D)),
                  w_in_columns(OFF_F, HGRN_WIDTH), w_in_columns(OFF_I, HGRN_WIDTH),
                  w_in_columns(OFF_POOL, POOL_WIDTH), whole((2, HGRN_WIDTH))],
        out_specs=(whole((HEADS, HEAD_DIM, HEAD_DIM)), whole((N_META, POOL_WIDTH))),
        out_shape=(jax.ShapeDtypeStruct((HEADS, HEAD_DIM, HEAD_DIM), F32),
                   jax.ShapeDtypeStruct((N_META, POOL_WIDTH), F32)),
        compiler_params=pltpu.CompilerParams(vmem_limit_bytes=VMEM_LIMIT_BYTES),
        name="meta_state",
    )(meta_tokens, g_mix, w_in_bf, w_in_bf, w_in_bf, lb_raw)

    T = MIXER_BLOCK
    steps = S // T
    ffn_weight_specs = [_row_slabs(D, d_ff, steps), _row_slabs(D, d_ff, steps),
                        _row_slabs(d_ff, D, steps)]
    h1, w_gate_bf, w_up_bf, w_down_bf = pl.pallas_call(
        _mixer_kernel,
        grid=(steps,),
        in_specs=[
            pl.BlockSpec((B, T, D), lambda t: (0, t, 0)),
            _resident((1, D)),
            _resident((D, IN_WIDTH)),
            _resident((2, HGRN_WIDTH)),
            _resident((1, HEAD_DIM)),
            _resident((len(POOL_WINDOWS), POOL_GROUP_DIM, POOL_GROUP_DIM)),
            _resident((1, POOL_WIDTH)),
            _resident((HGRN_WIDTH, D)),
            _resident((POOL_WIDTH, D)),
            _resident((D, D)),
            _resident((HEADS, HEAD_DIM, HEAD_DIM)),
            _resident((N_META, POOL_WIDTH)),
        ] + ffn_weight_specs,
        out_specs=[pl.BlockSpec((B, T, D), lambda t: (0, t, 0))] + ffn_weight_specs,
        out_shape=(jax.ShapeDtypeStruct((B, S, D), F32),
                   jax.ShapeDtypeStruct((D, d_ff), BF16),
                   jax.ShapeDtypeStruct((D, d_ff), BF16),
                   jax.ShapeDtypeStruct((d_ff, D), BF16)),
        scratch_shapes=[
            pltpu.VMEM((B * HEADS, HEAD_DIM, HEAD_DIM), F32),
            pltpu.VMEM((B, POOL_HISTORY, POOL_WIDTH), F32),
        ],
        compiler_params=pltpu.CompilerParams(
            dimension_semantics=("arbitrary",),
            vmem_limit_bytes=VMEM_LIMIT_BYTES),
        name="mixer",
    )(x, g_mix, w_in_bf, lb_raw, hgrn_norm_g[0].reshape(1, HEAD_DIM),
      pool_w[0].astype(BF16), pool_scale[0].reshape(1, POOL_WIDTH),
      w_branch_a[0].astype(BF16), w_branch_b[0].astype(BF16), w_out[0].astype(BF16),
      state0, hist0, w_ffn_gate[0], w_ffn_up[0], w_ffn_down[0])

    TM = FFN_BLOCK
    out = pl.pallas_call(
        _ffn_kernel,
        grid=(B * S // TM,),
        in_specs=[
            pl.BlockSpec((TM, D), lambda i: (i, 0)),
            _resident((1, D)),
            _resident((D, d_ff)),
            _resident((D, d_ff)),
            _resident((d_ff, D)),
            _resident((1, D)),
        ],
        out_specs=pl.BlockSpec((TM, D), lambda i: (i, 0)),
        out_shape=jax.ShapeDtypeStruct((B * S, D), F32),
        compiler_params=pltpu.CompilerParams(
            dimension_semantics=("arbitrary",),
            vmem_limit_bytes=VMEM_LIMIT_BYTES),
        name="ffn",
    )(h1.reshape(B * S, D), norm_ffn_g[0].reshape(1, D), w_gate_bf, w_up_bf, w_down_bf,
      norm_final_g.reshape(1, D))
    return out.reshape(B, S, D)
```

```python
import jax
import jax.numpy as jnp
from jax import lax
from jax.experimental import pallas as pl
from jax.experimental.pallas import tpu as pltpu

D_MODEL = 1024
N_META = 16
HEADS = 8
HEAD_DIM = 128
HGRN_WIDTH = HEADS * HEAD_DIM
POOL_WINDOWS = (2, 4, 8, 16)
POOL_GROUP_DIM = 128
POOL_WIDTH = len(POOL_WINDOWS) * POOL_GROUP_DIM
POOL_HISTORY = 16
EPS = 1e-6

OFF_Q = 0
OFF_F = OFF_Q + HGRN_WIDTH
OFF_I = OFF_F + HGRN_WIDTH
OFF_OG = OFF_I + HGRN_WIDTH
OFF_POOL = OFF_OG + HGRN_WIDTH
OFF_GA = OFF_POOL + POOL_WIDTH
OFF_GB = OFF_GA + D_MODEL
IN_WIDTH = OFF_GB + D_MODEL

SUBLANES = 8
BF16_ROWS = 2 * SUBLANES
MIXER_BLOCK = 256
SCORE_CHUNK = 128
DENSE_COLS = 256
MID_LEVEL = 8
FFN_BLOCK = 1024
FFN_ROWS = 256
VMEM_BYTES = 64 * 1024 * 1024
VMEM_TEMPORARIES_BYTES = 16 * 1024 * 1024

F32 = jnp.float32
BF16 = jnp.bfloat16


def _rmsnorm(x, g):
    ms = jnp.mean(x * x, axis=-1, keepdims=True)
    return x * lax.rsqrt(ms + EPS) * g


def _sigmoid(x):
    return 0.5 * jnp.tanh(0.5 * x) + 0.5


def _silu(x):
    h = 0.5 * x
    return h * jnp.tanh(h) + h


def _dot(a, b):
    return jnp.dot(a, b, preferred_element_type=F32)


def _dot_nt(a, b):
    return lax.dot_general(a, b, (((1,), (1,)), ((), ())), preferred_element_type=F32)


def _dot_tn(a, b):
    return lax.dot_general(a, b, (((0,), (0,)), ((), ())), preferred_element_type=F32)


def _lower_bound(lb_raw):
    r0, r1 = lb_raw[0:1, :], lb_raw[1:2, :]
    m = jnp.maximum(r0, r1)
    e0, e1 = jnp.exp(r0 - m), jnp.exp(r1 - m)
    return e0 / (e0 + e1)


def _cumsum_rows(g, tri):
    g1 = g.astype(BF16)
    rem = g - g1.astype(F32)
    g2 = rem.astype(BF16)
    g3 = (rem - g2.astype(F32)).astype(BF16)
    return _dot(tri, g1) + _dot(tri, g2) + _dot(tri, g3)


def _tri(n):
    r = lax.broadcasted_iota(jnp.int32, (n, n), 0)
    c = lax.broadcasted_iota(jnp.int32, (n, n), 1)
    return (c <= r).astype(BF16)


def _forget_gate(zf, lb):
    f = lb + (1.0 - lb) * _sigmoid(zf)
    return 1.0 - f, f, jnp.log2(f)


def _meta_state_kernel(meta_ref, g_ref, wf_ref, wi_ref, wp_ref, lb_raw_ref,
                       state_ref, hist_ref):
    n = _rmsnorm(meta_ref[...], g_ref[...]).astype(BF16)
    lb = _lower_bound(lb_raw_ref[...])
    k, _, g = _forget_gate(_dot(n, wf_ref[...]), lb)
    v = _dot(n, wi_ref[...])
    hist_ref[...] = _dot(n, wp_ref[...])
    b = _cumsum_rows(g, _tri(N_META))
    k_end = (k * jnp.exp2(b[N_META - 1:N_META, :] - b)).astype(BF16)
    v = v.astype(BF16)
    for h in range(HEADS):
        sl = slice(h * HEAD_DIM, (h + 1) * HEAD_DIM)
        state_ref[h] = _dot_tn(v[:, sl], k_end[:, sl])


def _sibling_boundary(b, m, sub):
    T = b.shape[0]
    if m >= SUBLANES:
        nb = T // (2 * m)
        r = b.reshape(nb, 2 * m, HEAD_DIM)[:, m - 1:m, :]
        return jnp.broadcast_to(r, (nb, 2 * m, HEAD_DIM)).reshape(T, HEAD_DIM)
    groups = b.reshape(T // SUBLANES, SUBLANES, HEAD_DIM)

    def bcast(i):
        r = jnp.broadcast_to(groups[:, i:i + 1, :], groups.shape)
        return r.reshape(T, HEAD_DIM)

    r = bcast(m - 1)
    for first in range(2 * m, SUBLANES, 2 * m):
        r = jnp.where(sub >= first, bcast(first + m - 1), r)
    return r


def _level_constants(T):
    row = lax.broadcasted_iota(jnp.int32, (T, HEAD_DIM), 0)
    r_i = lax.broadcasted_iota(jnp.int32, (SCORE_CHUNK, SCORE_CHUNK), 0)
    c_i = lax.broadcasted_iota(jnp.int32, (SCORE_CHUNK, SCORE_CHUNK), 1)
    consts = {}
    m = 1
    while m < T:
        later = ((row // m) % 2) == 1
        sign = jnp.where(later, 1.0, -1.0)
        mask = None
        if m < SCORE_CHUNK:
            mask = (((r_i // (2 * m)) == (c_i // (2 * m)))
                    & ((r_i // m) % 2 == 1) & ((c_i // m) % 2 == 0))
        consts[m] = (later, sign, mask)
        m *= 2
    return row % SUBLANES, consts


def _hgrn_scores(q, k, f, b, state_t, sub, consts, tick):
    T = q.shape[0]
    n_chunks = T // SCORE_CHUNK
    o_prev = _dot_nt((q * jnp.exp2(b)).astype(BF16), state_t.astype(BF16))

    blocks = [[None] * n_chunks for _ in range(n_chunks)]
    m = 2
    while m < T:
        later, sign, mask = consts[m]
        r = _sibling_boundary(b, m, sub)
        z32 = jnp.where(later, q, k) * jnp.exp2((b - r) * sign)
        z = z32.astype(BF16)
        zc = [z[c * SCORE_CHUNK:(c + 1) * SCORE_CHUNK] for c in range(n_chunks)]
        if m < SCORE_CHUNK:
            zt = [z32[c * SCORE_CHUNK:(c + 1) * SCORE_CHUNK].T.astype(BF16)
                  for c in range(n_chunks)]
        if m < BF16_ROWS:
            for c in range(n_chunks):
                s = _dot(zc[c], zt[c])
                blocks[c][c] = jnp.where(mask, s, 0.0 if blocks[c][c] is None else blocks[c][c])
        elif m < SCORE_CHUNK:
            pairs = SCORE_CHUNK // (2 * m)
            for c in range(n_chunks):
                queries = jnp.concatenate(
                    [zc[c][(2 * j + 1) * m:(2 * j + 2) * m] for j in range(pairs)], axis=0)
                s = _dot(queries, zt[c])
                prev = blocks[c][c]
                rows = []
                for j in range(pairs):
                    lo, hi = (2 * j + 1) * m, (2 * j + 2) * m
                    rows.append(prev[lo - m:lo])
                    rows.append(jnp.where(mask[lo:hi], s[j * m:(j + 1) * m], prev[lo:hi]))
                blocks[c][c] = jnp.concatenate(rows, axis=0)
        else:
            span = m // SCORE_CHUNK
            for i in range(n_chunks):
                if (i // span) % 2 == 1:
                    for j in range((i // span - 1) * span, (i // span) * span):
                        blocks[i][j] = _dot_nt(zc[i], zc[j])
        if m == MID_LEVEL:
            tick()
        m *= 2
    return o_prev, blocks


def _previous_row(x):
    T = x.shape[0]
    groups = x.reshape(T // SUBLANES, SUBLANES, HEAD_DIM)
    return pltpu.roll(groups, 1, 1).reshape(T, HEAD_DIM)


def _hgrn_output(o_prev, blocks, q, k, f, v, b, state_t, odd):
    T = q.shape[0]
    n_chunks = T // SCORE_CHUNK
    v_bf = v.astype(BF16)
    strips = []
    for i in range(n_chunks):
        a = jnp.concatenate(blocks[i][:i + 1], axis=1) if i else blocks[0][0]
        strips.append(_dot(a.astype(BF16), v_bf[:(i + 1) * SCORE_CHUNK]))
    o = o_prev + (jnp.concatenate(strips, axis=0) if n_chunks > 1 else strips[0])
    o = o + jnp.sum(q * k, axis=-1, keepdims=True) * v
    pair = jnp.where(odd, q * f, 0.0) * _previous_row(k)
    o = o + jnp.sum(pair, axis=-1, keepdims=True) * _previous_row(v)

    b_end = b[T - 1:T, :]
    k_end = (k * jnp.exp2(b_end - b)).astype(BF16)
    new_state_t = state_t * jnp.exp2(b_end) + _dot_tn(v_bf, k_end)
    return o, new_state_t


def _mixer_kernel(x_ref, g_mix_ref, w_in_ref, lb_raw_ref, g_hgrn_ref, pool_w_ref, pool_scale_ref,
                  w_a_ref, w_b_ref, w_out_ref, state0_ref, hist0_ref,
                  ffn_gate_ref, ffn_up_ref, ffn_down_ref,
                  out_ref, ffn_gate_bf_ref, ffn_up_bf_ref, ffn_down_bf_ref,
                  state_ref, hist_ref):
    T = MIXER_BLOCK
    B = x_ref.shape[0]

    @pl.when(pl.program_id(0) == 0)
    def _():
        for b in range(B):
            for h in range(HEADS):
                state_ref[b * HEADS + h] = state0_ref[h]
            hist_ref[b] = hist0_ref[...]

    ffn_gate_bf_ref[...] = ffn_gate_ref[...].astype(BF16)
    ffn_up_bf_ref[...] = ffn_up_ref[...].astype(BF16)
    ffn_down_bf_ref[...] = ffn_down_ref[...].astype(BF16)

    xs = [x_ref[b] for b in range(B)]
    ns = [_rmsnorm(x, g_mix_ref[...]).astype(BF16) for x in xs]
    lb = _lower_bound(lb_raw_ref[...])
    tri = _tri(T)
    sub, consts = _level_constants(T)
    g_hgrn = g_hgrn_ref[...]
    PAIR = 2 * HEAD_DIM

    queue = []

    def tick(n=1):
        for _ in range(n):
            if queue:
                queue.pop(0)[1]()

    def flush(tag=None):
        while queue if tag is None else any(t == tag for t, _ in queue):
            queue.pop(0)[1]()

    def enqueue_dot(tag, lhs, rhs_ref, col0, n_cols):
        blocks = [None] * (n_cols // DENSE_COLS)
        for cb in range(len(blocks)):
            def piece(cb=cb):
                cols = slice(col0 + cb * DENSE_COLS, col0 + (cb + 1) * DENSE_COLS)
                blocks[cb] = _dot(lhs(), rhs_ref[:, cols])
            queue.append((tag, piece))
        return blocks

    def assemble(blocks):
        return jnp.concatenate(blocks, axis=1) if len(blocks) > 1 else blocks[0]

    def enqueue_projection(tag, b, offset, width):
        return enqueue_dot(tag, lambda: ns[b], w_in_ref, offset, width)

    def enqueue_pair(b, p):
        return tuple(enqueue_projection(("pair", b, p), b, off + p * PAIR, PAIR)
                     for off in (OFF_Q, OFF_F, OFF_I, OFF_OG))

    def hgrn_pair(b, p, zq, zf, zi, zog):
        k, f, g = _forget_gate(zf, lb[:, p * PAIR:(p + 1) * PAIR])
        bsum = _cumsum_rows(g, tri)
        tick(2)
        q = _silu(zq)
        og = _silu(zog)
        heads = [(b * HEADS + 2 * p + j, slice(j * HEAD_DIM, (j + 1) * HEAD_DIM))
                 for j in range(2)]
        scores = []
        for h, sl in heads:
            scores.append(_hgrn_scores(q[:, sl], k[:, sl], f[:, sl], bsum[:, sl], state_ref[h],
                                       sub, consts, tick))
            tick()
        tick(2)
        ys = []
        for (h, sl), (o_prev, blocks) in zip(heads, scores):
            o, new_state = _hgrn_output(o_prev, blocks, q[:, sl], k[:, sl], f[:, sl], zi[:, sl],
                                        bsum[:, sl], state_ref[h], consts[1][0])
            state_ref[h] = new_state
            ys.append((_rmsnorm(o, g_hgrn) * og[:, sl]).astype(BF16))
        return jnp.concatenate(ys, axis=1)

    def pool_branch(b, u):
        ext = jnp.concatenate([hist_ref[b], u], axis=0)
        hist_ref[b] = u[T - POOL_HISTORY:, :]
        acc = ext
        span = 1
        ys = []
        for gi, w in enumerate(POOL_WINDOWS):
            while span < w:
                acc = acc + pltpu.roll(acc, span, 0)
                span *= 2
            sl = slice(gi * POOL_GROUP_DIM, (gi + 1) * POOL_GROUP_DIM)
            pooled = acc[POOL_HISTORY:, sl] * (1.0 / w) - u[:, sl]
            y = _dot(pooled.astype(BF16), pool_w_ref[gi]) * pool_scale_ref[:, sl]
            ys.append(y.astype(BF16))
        return jnp.concatenate(ys, axis=1)

    def enqueue_outputs(b, ya, z_pool, z_gb, z_ga):
        n_blocks = D_MODEL // DENSE_COLS
        hold = {}

        def start():
            hold["ya"] = jnp.concatenate(ya, axis=1)
            hold["yb"] = pool_branch(b, assemble(z_pool))
        queue.append((("out", b), start))
        branch_a = enqueue_dot(("out", b), lambda: hold["ya"], w_a_ref, 0, D_MODEL)
        branch_b = enqueue_dot(("out", b), lambda: hold["yb"], w_b_ref, 0, D_MODEL)

        def merge():
            merged = [_sigmoid(z_ga[cb]) * branch_a[cb]
                      + _sigmoid(z_gb[cb]) * branch_b[cb] for cb in range(n_blocks)]
            hold["merged"] = jnp.concatenate(merged, axis=1).astype(BF16)
        queue.append((("out", b), merge))
        for cb in range(n_blocks):
            def piece(cb=cb):
                cols = slice(cb * DENSE_COLS, (cb + 1) * DENSE_COLS)
                out_ref[b, :, cols] = xs[b][:, cols] + _dot(hold["merged"], w_out_ref[:, cols])
            queue.append((("out", b), piece))

    n_pairs = HEADS // 2
    z = {(0, 0): enqueue_pair(0, 0)}
    flush()
    others = {}
    for b in range(B):
        for p in range(1, n_pairs):
            z[(b, p)] = enqueue_pair(b, p)
        if b + 1 < B:
            z[(b + 1, 0)] = enqueue_pair(b + 1, 0)
        others[b] = tuple(enqueue_projection(("other", b), b, off, width) for off, width in
                          ((OFF_POOL, POOL_WIDTH), (OFF_GB, D_MODEL), (OFF_GA, D_MODEL)))
    for b in range(B):
        ya = []
        for p in range(n_pairs):
            flush(("pair", b, p))
            ya.append(hgrn_pair(b, p, *(assemble(blocks) for blocks in z[(b, p)])))
        enqueue_outputs(b, ya, *others[b])
    flush()


def _ffn_kernel(h_ref, g_ffn_ref, w_gate_ref, w_up_ref, w_down_ref, g_final_ref, out_ref):
    groups = [slice(r * FFN_ROWS, (r + 1) * FFN_ROWS) for r in range(FFN_BLOCK // FFN_ROWS)]

    def normed(rows):
        return _rmsnorm(h_ref[rows, :], g_ffn_ref[...]).astype(BF16)

    def finish(rows, down):
        out_ref[rows, :] = _rmsnorm(h_ref[rows, :] + down, g_final_ref[...])

    def activation(n):
        d_ff = w_gate_ref.shape[1]
        acts = []
        for c0 in range(0, d_ff, DENSE_COLS):
            cols = slice(c0, c0 + DENSE_COLS)
            gate = _dot(n, w_gate_ref[:, cols])
            up = _dot(n, w_up_ref[:, cols])
            acts.append((_silu(gate) * up).astype(BF16))
        return jnp.concatenate(acts, axis=1)

    n = normed(groups[0])
    pending = None
    for i, rows in enumerate(groups):
        act = activation(n)
        if pending is not None:
            finish(*pending)
        if i + 1 < len(groups):
            n = normed(groups[i + 1])
        pending = (rows, _dot(act, w_down_ref[...]))
    finish(*pending)


def _resident(shape):
    zeros = (0,) * len(shape)
    return pl.BlockSpec(shape, lambda *_: zeros, pipeline_mode=pl.Buffered(1))


def _vmem_limit(resident, streamed):
    def nbytes(arrays):
        total = 0
        for shape, dtype in arrays:
            size = jnp.dtype(dtype).itemsize
            for dim in shape:
                size *= dim
            total += size
        return total

    limit = nbytes(resident) + 2 * nbytes(streamed) + VMEM_TEMPORARIES_BYTES
    assert limit <= VMEM_BYTES, "blocks and resident weights do not fit VMEM"
    return limit


def _row_slabs(rows, cols, steps):
    slab = next(r for r in range(BF16_ROWS, rows + 1, BF16_ROWS)
                if rows % r == 0 and rows // r <= steps)
    last = rows // slab - 1
    return pl.BlockSpec((slab, cols), lambda t: (jnp.minimum(t, last), 0))


def kernel(x, meta_tokens, norm_mix_g, w_in, lb_raw, hgrn_norm_g, pool_w, pool_scale,
           w_branch_a, w_branch_b, w_out, norm_ffn_g, w_ffn_gate, w_ffn_up, w_ffn_down,
           norm_final_g):
    B, S, D = x.shape
    assert D == D_MODEL and S % MIXER_BLOCK == 0 and (B * S) % FFN_BLOCK == 0
    assert w_in.shape == (1, D_MODEL, IN_WIDTH) and meta_tokens.shape == (N_META, D_MODEL)
    d_ff = w_ffn_gate.shape[-1]

    w_in_bf = w_in[0].astype(BF16)
    g_mix = norm_mix_g[0].reshape(1, D)
    lb_raw = lb_raw.astype(F32)

    def whole(shape):
        zeros = (0,) * len(shape)
        return pl.BlockSpec(shape, lambda i: zeros)

    def w_in_columns(offset, width):
        assert offset % width == 0
        return pl.BlockSpec((D, width), lambda i: (0, offset // width))

    state0, hist0 = pl.pallas_call(
        _meta_state_kernel,
        grid=(1,),
        in_specs=[whole((N_META, D)), whole((1, D)),
                  w_in_columns(OFF_F, HGRN_WIDTH), w_in_columns(OFF_I, HGRN_WIDTH),
                  w_in_columns(OFF_POOL, POOL_WIDTH), whole((2, HGRN_WIDTH))],
        out_specs=(whole((HEADS, HEAD_DIM, HEAD_DIM)), whole((N_META, POOL_WIDTH))),
        out_shape=(jax.ShapeDtypeStruct((HEADS, HEAD_DIM, HEAD_DIM), F32),
                   jax.ShapeDtypeStruct((N_META, POOL_WIDTH), F32)),
        compiler_params=pltpu.CompilerParams(vmem_limit_bytes=_vmem_limit(
            [], [((D, 2 * HGRN_WIDTH + POOL_WIDTH), BF16), ((N_META, D), F32)])),
        name="meta_state",
    )(meta_tokens, g_mix, w_in_bf, w_in_bf, w_in_bf, lb_raw)

    T = MIXER_BLOCK
    steps = S // T
    ffn_weight_specs = [_row_slabs(D, d_ff, steps), _row_slabs(D, d_ff, steps),
                        _row_slabs(d_ff, D, steps)]
    h1, w_gate_bf, w_up_bf, w_down_bf = pl.pallas_call(
        _mixer_kernel,
        grid=(steps,),
        in_specs=[
            pl.BlockSpec((B, T, D), lambda t: (0, t, 0)),
            _resident((1, D)),
            _resident((D, IN_WIDTH)),
            _resident((2, HGRN_WIDTH)),
            _resident((1, HEAD_DIM)),
            _resident((len(POOL_WINDOWS), POOL_GROUP_DIM, POOL_GROUP_DIM)),
            _resident((1, POOL_WIDTH)),
            _resident((HGRN_WIDTH, D)),
            _resident((POOL_WIDTH, D)),
            _resident((D, D)),
            _resident((HEADS, HEAD_DIM, HEAD_DIM)),
            _resident((N_META, POOL_WIDTH)),
        ] + ffn_weight_specs,
        out_specs=[pl.BlockSpec((B, T, D), lambda t: (0, t, 0))] + ffn_weight_specs,
        out_shape=(jax.ShapeDtypeStruct((B, S, D), F32),
                   jax.ShapeDtypeStruct((D, d_ff), BF16),
                   jax.ShapeDtypeStruct((D, d_ff), BF16),
                   jax.ShapeDtypeStruct((d_ff, D), BF16)),
        scratch_shapes=[
            pltpu.VMEM((B * HEADS, HEAD_DIM, HEAD_DIM), F32),
            pltpu.VMEM((B, POOL_HISTORY, POOL_WIDTH), F32),
        ],
        compiler_params=pltpu.CompilerParams(
            dimension_semantics=("arbitrary",),
            vmem_limit_bytes=_vmem_limit(
                [((D, IN_WIDTH + HGRN_WIDTH + POOL_WIDTH + D), BF16),
                 (((B + 1) * HEADS, HEAD_DIM, HEAD_DIM), F32)],
                [((B, T, D), F32)] * 2
                + [(spec.block_shape, dtype) for spec in ffn_weight_specs
                   for dtype in (F32, BF16)])),
        name="mixer",
    )(x, g_mix, w_in_bf, lb_raw, hgrn_norm_g[0].reshape(1, HEAD_DIM),
      pool_w[0].astype(BF16), pool_scale[0].reshape(1, POOL_WIDTH),
      w_branch_a[0].astype(BF16), w_branch_b[0].astype(BF16), w_out[0].astype(BF16),
      state0, hist0, w_ffn_gate[0], w_ffn_up[0], w_ffn_down[0])

    TM = FFN_BLOCK
    out = pl.pallas_call(
        _ffn_kernel,
        grid=(B * S // TM,),
        in_specs=[
            pl.BlockSpec((TM, D), lambda i: (i, 0)),
            _resident((1, D)),
            _resident((D, d_ff)),
            _resident((D, d_ff)),
            _resident((d_ff, D)),
            _resident((1, D)),
        ],
        out_specs=pl.BlockSpec((TM, D), lambda i: (i, 0)),
        out_shape=jax.ShapeDtypeStruct((B * S, D), F32),
        compiler_params=pltpu.CompilerParams(
            dimension_semantics=("arbitrary",),
            vmem_limit_bytes=_vmem_limit([((D, 3 * d_ff), BF16)], [((TM, D), F32)] * 2)),
        name="ffn",
    )(h1.reshape(B * S, D), norm_ffn_g[0].reshape(1, D), w_gate_bf, w_up_bf, w_down_bf,
      norm_final_g.reshape(1, D))
    return out.reshape(B, S, D)
```

```python
import jax
import jax.numpy as jnp
from jax import lax
from jax.experimental import pallas as pl
from jax.experimental.pallas import tpu as pltpu

D_MODEL = 1024
N_META = 16
HEADS = 8
HEAD_DIM = 128
HGRN_WIDTH = HEADS * HEAD_DIM
POOL_WINDOWS = (2, 4, 8, 16)
POOL_GROUP_DIM = 128
POOL_WIDTH = len(POOL_WINDOWS) * POOL_GROUP_DIM
POOL_HISTORY = 16
EPS = 1e-6

OFF_Q = 0
OFF_F = OFF_Q + HGRN_WIDTH
OFF_I = OFF_F + HGRN_WIDTH
OFF_OG = OFF_I + HGRN_WIDTH
OFF_POOL = OFF_OG + HGRN_WIDTH
OFF_GA = OFF_POOL + POOL_WIDTH
OFF_GB = OFF_GA + D_MODEL
IN_WIDTH = OFF_GB + D_MODEL

SUBLANES = 8
BF16_ROWS = 2 * SUBLANES
MIXER_BLOCK = 256
SCORE_CHUNK = 128
DENSE_COLS = 256
MID_LEVEL = 8
FFN_BLOCK = 1024
FFN_ROWS = 256
VMEM_BYTES = 64 * 1024 * 1024
VMEM_TEMPORARIES_BYTES = 16 * 1024 * 1024

F32 = jnp.float32
BF16 = jnp.bfloat16


def _rmsnorm(x, g):
    ms = jnp.mean(x * x, axis=-1, keepdims=True)
    return x * lax.rsqrt(ms + EPS) * g


def _sigmoid(x):
    return 0.5 * jnp.tanh(0.5 * x) + 0.5


def _silu(x):
    h = 0.5 * x
    return h * jnp.tanh(h) + h


def _dot(a, b):
    return jnp.dot(a, b, preferred_element_type=F32)


def _dot_nt(a, b):
    return lax.dot_general(a, b, (((1,), (1,)), ((), ())), preferred_element_type=F32)


def _dot_tn(a, b):
    return lax.dot_general(a, b, (((0,), (0,)), ((), ())), preferred_element_type=F32)


def _lower_bound(lb_raw):
    r0, r1 = lb_raw[0:1, :], lb_raw[1:2, :]
    m = jnp.maximum(r0, r1)
    e0, e1 = jnp.exp(r0 - m), jnp.exp(r1 - m)
    return e0 / (e0 + e1)


def _cumsum_rows(g, tri):
    g1 = g.astype(BF16)
    rem = g - g1.astype(F32)
    g2 = rem.astype(BF16)
    g3 = (rem - g2.astype(F32)).astype(BF16)
    return _dot(tri, g1) + _dot(tri, g2) + _dot(tri, g3)


def _tri(n):
    r = lax.broadcasted_iota(jnp.int32, (n, n), 0)
    c = lax.broadcasted_iota(jnp.int32, (n, n), 1)
    return (c <= r).astype(BF16)


def _forget_gate(zf, lb):
    f = lb + (1.0 - lb) * _sigmoid(zf)
    return 1.0 - f, f, jnp.log2(f)


def _meta_state_kernel(meta_ref, g_ref, wf_ref, wi_ref, wp_ref, lb_raw_ref,
                       state_ref, hist_ref):
    n = _rmsnorm(meta_ref[...], g_ref[...]).astype(BF16)
    lb = _lower_bound(lb_raw_ref[...])
    k, _, g = _forget_gate(_dot(n, wf_ref[...]), lb)
    v = _dot(n, wi_ref[...])
    hist_ref[...] = _dot(n, wp_ref[...])
    b = _cumsum_rows(g, _tri(N_META))
    k_end = (k * jnp.exp2(b[N_META - 1:N_META, :] - b)).astype(BF16)
    v = v.astype(BF16)
    for h in range(HEADS):
        sl = slice(h * HEAD_DIM, (h + 1) * HEAD_DIM)
        state_ref[h] = _dot_tn(v[:, sl], k_end[:, sl])


def _sibling_boundary(b, m, sub):
    T = b.shape[0]
    if m >= SUBLANES:
        nb = T // (2 * m)
        r = b.reshape(nb, 2 * m, HEAD_DIM)[:, m - 1:m, :]
        return jnp.broadcast_to(r, (nb, 2 * m, HEAD_DIM)).reshape(T, HEAD_DIM)
    groups = b.reshape(T // SUBLANES, SUBLANES, HEAD_DIM)

    def bcast(i):
        r = jnp.broadcast_to(groups[:, i:i + 1, :], groups.shape)
        return r.reshape(T, HEAD_DIM)

    r = bcast(m - 1)
    for first in range(2 * m, SUBLANES, 2 * m):
        r = jnp.where(sub >= first, bcast(first + m - 1), r)
    return r


def _level_constants(T):
    row = lax.broadcasted_iota(jnp.int32, (T, HEAD_DIM), 0)
    r_i = lax.broadcasted_iota(jnp.int32, (SCORE_CHUNK, SCORE_CHUNK), 0)
    c_i = lax.broadcasted_iota(jnp.int32, (SCORE_CHUNK, SCORE_CHUNK), 1)
    consts = {}
    m = 1
    while m < T:
        later = ((row // m) % 2) == 1
        sign = jnp.where(later, 1.0, -1.0)
        mask = None
        if m < SCORE_CHUNK:
            mask = (((r_i // (2 * m)) == (c_i // (2 * m)))
                    & ((r_i // m) % 2 == 1) & ((c_i // m) % 2 == 0))
        consts[m] = (later, sign, mask)
        m *= 2
    return row % SUBLANES, consts


def _hgrn_scores(q, k, f, b, state_t, sub, consts, tick):
    T = q.shape[0]
    n_chunks = T // SCORE_CHUNK
    o_prev = _dot((q * jnp.exp2(b)).astype(BF16), state_t.T.astype(BF16))

    blocks = [[None] * n_chunks for _ in range(n_chunks)]
    m = 2
    while m < T:
        later, sign, mask = consts[m]
        r = _sibling_boundary(b, m, sub)
        z32 = jnp.where(later, q, k) * jnp.exp2((b - r) * sign)
        z = z32.astype(BF16)
        zc = [z[c * SCORE_CHUNK:(c + 1) * SCORE_CHUNK] for c in range(n_chunks)]
        zt = [z32[c * SCORE_CHUNK:(c + 1) * SCORE_CHUNK].T.astype(BF16)
              for c in range(n_chunks - (m >= SCORE_CHUNK))]
        if m < BF16_ROWS:
            for c in range(n_chunks):
                s = _dot(zc[c], zt[c])
                blocks[c][c] = jnp.where(mask, s, 0.0 if blocks[c][c] is None else blocks[c][c])
        elif m < SCORE_CHUNK:
            pairs = SCORE_CHUNK // (2 * m)
            for c in range(n_chunks):
                queries = jnp.concatenate(
                    [zc[c][(2 * j + 1) * m:(2 * j + 2) * m] for j in range(pairs)], axis=0)
                s = _dot(queries, zt[c])
                prev = blocks[c][c]
                rows = []
                for j in range(pairs):
                    lo, hi = (2 * j + 1) * m, (2 * j + 2) * m
                    rows.append(prev[lo - m:lo])
                    rows.append(jnp.where(mask[lo:hi], s[j * m:(j + 1) * m], prev[lo:hi]))
                blocks[c][c] = jnp.concatenate(rows, axis=0)
        else:
            span = m // SCORE_CHUNK
            for i in range(n_chunks):
                if (i // span) % 2 == 1:
                    for j in range((i // span - 1) * span, (i // span) * span):
                        blocks[i][j] = _dot(zc[i], zt[j])
        if m == MID_LEVEL:
            tick()
        m *= 2
    return o_prev, blocks


def _previous_row(x):
    T = x.shape[0]
    groups = x.reshape(T // SUBLANES, SUBLANES, HEAD_DIM)
    return pltpu.roll(groups, 1, 1).reshape(T, HEAD_DIM)


def _hgrn_output(o_prev, blocks, q, k, f, v, b, state_t, odd):
    T = q.shape[0]
    n_chunks = T // SCORE_CHUNK
    v_bf = v.astype(BF16)
    strips = []
    for i in range(n_chunks):
        a = jnp.concatenate(blocks[i][:i + 1], axis=1) if i else blocks[0][0]
        strips.append(_dot(a.astype(BF16), v_bf[:(i + 1) * SCORE_CHUNK]))
    o = o_prev + (jnp.concatenate(strips, axis=0) if n_chunks > 1 else strips[0])
    o = o + jnp.sum(q * k, axis=-1, keepdims=True) * v
    pair = jnp.where(odd, q * f, 0.0) * _previous_row(k)
    o = o + jnp.sum(pair, axis=-1, keepdims=True) * _previous_row(v)

    b_end = b[T - 1:T, :]
    k_end = (k * jnp.exp2(b_end - b)).astype(BF16)
    new_state_t = state_t * jnp.exp2(b_end) + _dot_tn(v_bf, k_end)
    return o, new_state_t


def _mixer_kernel(x_ref, g_mix_ref, w_in_ref, lb_raw_ref, g_hgrn_ref, pool_w_ref, pool_scale_ref,
                  w_a_ref, w_b_ref, w_out_ref, state0_ref, hist0_ref,
                  ffn_gate_ref, ffn_up_ref, ffn_down_ref,
                  out_ref, ffn_gate_bf_ref, ffn_up_bf_ref, ffn_down_bf_ref,
                  state_ref, hist_ref):
    T = MIXER_BLOCK
    B = x_ref.shape[0]

    @pl.when(pl.program_id(0) == 0)
    def _():
        for b in range(B):
            for h in range(HEADS):
                state_ref[b * HEADS + h] = state0_ref[h]
            hist_ref[b] = hist0_ref[...]

    ffn_gate_bf_ref[...] = ffn_gate_ref[...].astype(BF16)
    ffn_up_bf_ref[...] = ffn_up_ref[...].astype(BF16)
    ffn_down_bf_ref[...] = ffn_down_ref[...].astype(BF16)

    xs = [x_ref[b] for b in range(B)]
    ns = [_rmsnorm(x, g_mix_ref[...]).astype(BF16) for x in xs]
    lb = _lower_bound(lb_raw_ref[...])
    tri = _tri(T)
    sub, consts = _level_constants(T)
    g_hgrn = g_hgrn_ref[...]
    PAIR = 2 * HEAD_DIM

    queue = []

    def tick(n=1):
        for _ in range(n):
            if queue:
                queue.pop(0)[1]()

    def flush(tag=None):
        while queue if tag is None else any(t == tag for t, _ in queue):
            queue.pop(0)[1]()

    def enqueue_dot(tag, lhs, rhs_ref, col0, n_cols):
        blocks = [None] * (n_cols // DENSE_COLS)
        for cb in range(len(blocks)):
            def piece(cb=cb):
                cols = slice(col0 + cb * DENSE_COLS, col0 + (cb + 1) * DENSE_COLS)
                blocks[cb] = _dot(lhs(), rhs_ref[:, cols])
            queue.append((tag, piece))
        return blocks

    def assemble(blocks):
        return jnp.concatenate(blocks, axis=1) if len(blocks) > 1 else blocks[0]

    def enqueue_projection(tag, b, offset, width):
        return enqueue_dot(tag, lambda: ns[b], w_in_ref, offset, width)

    def enqueue_pair(b, p):
        return tuple(enqueue_projection(("pair", b, p), b, off + p * PAIR, PAIR)
                     for off in (OFF_Q, OFF_F, OFF_I, OFF_OG))

    def hgrn_pair(b, p, zq, zf, zi, zog):
        k, f, g = _forget_gate(zf, lb[:, p * PAIR:(p + 1) * PAIR])
        bsum = _cumsum_rows(g, tri)
        tick(2)
        q = _silu(zq)
        og = _silu(zog)
        heads = [(b * HEADS + 2 * p + j, slice(j * HEAD_DIM, (j + 1) * HEAD_DIM))
                 for j in range(2)]
        scores = []
        for h, sl in heads:
            scores.append(_hgrn_scores(q[:, sl], k[:, sl], f[:, sl], bsum[:, sl], state_ref[h],
                                       sub, consts, tick))
            tick()
        tick(2)
        ys = []
        for (h, sl), (o_prev, blocks) in zip(heads, scores):
            o, new_state = _hgrn_output(o_prev, blocks, q[:, sl], k[:, sl], f[:, sl], zi[:, sl],
                                        bsum[:, sl], state_ref[h], consts[1][0])
            state_ref[h] = new_state
            ys.append((_rmsnorm(o, g_hgrn) * og[:, sl]).astype(BF16))
        return jnp.concatenate(ys, axis=1)

    def pool_branch(b, u):
        ext = jnp.concatenate([hist_ref[b], u], axis=0)
        hist_ref[b] = u[T - POOL_HISTORY:, :]
        acc = ext
        span = 1
        ys = []
        for gi, w in enumerate(POOL_WINDOWS):
            while span < w:
                acc = acc + pltpu.roll(acc, span, 0)
                span *= 2
            sl = slice(gi * POOL_GROUP_DIM, (gi + 1) * POOL_GROUP_DIM)
            pooled = acc[POOL_HISTORY:, sl] * (1.0 / w) - u[:, sl]
            y = _dot(pooled.astype(BF16), pool_w_ref[gi]) * pool_scale_ref[:, sl]
            ys.append(y.astype(BF16))
        return jnp.concatenate(ys, axis=1)

    def enqueue_outputs(b, ya, z_pool, z_gb, z_ga):
        n_blocks = D_MODEL // DENSE_COLS
        hold = {}

        def start():
            hold["ya"] = jnp.concatenate(ya, axis=1)
            hold["yb"] = pool_branch(b, assemble(z_pool))
        queue.append((("out", b), start))
        branch_a = enqueue_dot(("out", b), lambda: hold["ya"], w_a_ref, 0, D_MODEL)
        branch_b = enqueue_dot(("out", b), lambda: hold["yb"], w_b_ref, 0, D_MODEL)

        def merge():
            merged = [_sigmoid(z_ga[cb]) * branch_a[cb]
                      + _sigmoid(z_gb[cb]) * branch_b[cb] for cb in range(n_blocks)]
            hold["merged"] = jnp.concatenate(merged, axis=1).astype(BF16)
        queue.append((("out", b), merge))
        for cb in range(n_blocks):
            def piece(cb=cb):
                cols = slice(cb * DENSE_COLS, (cb + 1) * DENSE_COLS)
                out_ref[b, :, cols] = xs[b][:, cols] + _dot(hold["merged"], w_out_ref[:, cols])
            queue.append((("out", b), piece))

    n_pairs = HEADS // 2
    z = {(0, 0): enqueue_pair(0, 0)}
    flush()
    others = {}
    for b in range(B):
        for p in range(1, n_pairs):
            z[(b, p)] = enqueue_pair(b, p)
        if b + 1 < B:
            z[(b + 1, 0)] = enqueue_pair(b + 1, 0)
        others[b] = tuple(enqueue_projection(("other", b), b, off, width) for off, width in
                          ((OFF_POOL, POOL_WIDTH), (OFF_GB, D_MODEL), (OFF_GA, D_MODEL)))
    for b in range(B):
        ya = []
        for p in range(n_pairs):
            flush(("pair", b, p))
            ya.append(hgrn_pair(b, p, *(assemble(blocks) for blocks in z[(b, p)])))
        enqueue_outputs(b, ya, *others[b])
    flush()


def _ffn_kernel(h_ref, g_ffn_ref, w_gate_ref, w_up_ref, w_down_ref, g_final_ref, out_ref):
    groups = [slice(r * FFN_ROWS, (r + 1) * FFN_ROWS) for r in range(FFN_BLOCK // FFN_ROWS)]

    def normed(rows):
        return _rmsnorm(h_ref[rows, :], g_ffn_ref[...]).astype(BF16)

    def finish(rows, down):
        out_ref[rows, :] = _rmsnorm(h_ref[rows, :] + down, g_final_ref[...])

    def activation(n):
        d_ff = w_gate_ref.shape[1]
        acts = []
        for c0 in range(0, d_ff, DENSE_COLS):
            cols = slice(c0, c0 + DENSE_COLS)
            gate = _dot(n, w_gate_ref[:, cols])
            up = _dot(n, w_up_ref[:, cols])
            acts.append((_silu(gate) * up).astype(BF16))
        return jnp.concatenate(acts, axis=1)

    n = normed(groups[0])
    pending = None
    for i, rows in enumerate(groups):
        act = activation(n)
        if pending is not None:
            finish(*pending)
        if i + 1 < len(groups):
            n = normed(groups[i + 1])
        pending = (rows, _dot(act, w_down_ref[...]))
    finish(*pending)


def _resident(shape):
    zeros = (0,) * len(shape)
    return pl.BlockSpec(shape, lambda *_: zeros, pipeline_mode=pl.Buffered(1))


def _vmem_limit(resident, streamed):
    def nbytes(arrays):
        total = 0
        for shape, dtype in arrays:
            size = jnp.dtype(dtype).itemsize
            for dim in shape:
                size *= dim
            total += size
        return total

    limit = nbytes(resident) + 2 * nbytes(streamed) + VMEM_TEMPORARIES_BYTES
    assert limit <= VMEM_BYTES, "blocks and resident weights do not fit VMEM"
    return limit


def _row_slabs(rows, cols, steps):
    slab = next(r for r in range(BF16_ROWS, rows + 1, BF16_ROWS)
                if rows % r == 0 and rows // r <= steps)
    last = rows // slab - 1
    return pl.BlockSpec((slab, cols), lambda t: (jnp.minimum(t, last), 0))


def kernel(x, meta_tokens, norm_mix_g, w_in, lb_raw, hgrn_norm_g, pool_w, pool_scale,
           w_branch_a, w_branch_b, w_out, norm_ffn_g, w_ffn_gate, w_ffn_up, w_ffn_down,
           norm_final_g):
    B, S, D = x.shape
    assert D == D_MODEL and S % MIXER_BLOCK == 0 and (B * S) % FFN_BLOCK == 0
    assert w_in.shape == (1, D_MODEL, IN_WIDTH) and meta_tokens.shape == (N_META, D_MODEL)
    d_ff = w_ffn_gate.shape[-1]

    w_in_bf = w_in[0].astype(BF16)
    g_mix = norm_mix_g[0].reshape(1, D)
    lb_raw = lb_raw.astype(F32)

    def whole(shape):
        zeros = (0,) * len(shape)
        return pl.BlockSpec(shape, lambda i: zeros)

    def w_in_columns(offset, width):
        assert offset % width == 0
        return pl.BlockSpec((D, width), lambda i: (0, offset // width))

    state0, hist0 = pl.pallas_call(
        _meta_state_kernel,
        grid=(1,),
        in_specs=[whole((N_META, D)), whole((1, D)),
                  w_in_columns(OFF_F, HGRN_WIDTH), w_in_columns(OFF_I, HGRN_WIDTH),
                  w_in_columns(OFF_POOL, POOL_WIDTH), whole((2, HGRN_WIDTH))],
        out_specs=(whole((HEADS, HEAD_DIM, HEAD_DIM)), whole((N_META, POOL_WIDTH))),
        out_shape=(jax.ShapeDtypeStruct((HEADS, HEAD_DIM, HEAD_DIM), F32),
                   jax.ShapeDtypeStruct((N_META, POOL_WIDTH), F32)),
        compiler_params=pltpu.CompilerParams(vmem_limit_bytes=_vmem_limit(
            [], [((D, 2 * HGRN_WIDTH + POOL_WIDTH), BF16), ((N_META, D), F32)])),
        name="meta_state",
    )(meta_tokens, g_mix, w_in_bf, w_in_bf, w_in_bf, lb_raw)

    T = MIXER_BLOCK
    steps = S // T
    ffn_weight_specs = [_row_slabs(D, d_ff, steps), _row_slabs(D, d_ff, steps),
                        _row_slabs(d_ff, D, steps)]
    h1, w_gate_bf, w_up_bf, w_down_bf = pl.pallas_call(
        _mixer_kernel,
        grid=(steps,),
        in_specs=[
            pl.BlockSpec((B, T, D), lambda t: (0, t, 0)),
            _resident((1, D)),
            _resident((D, IN_WIDTH)),
            _resident((2, HGRN_WIDTH)),
            _resident((1, HEAD_DIM)),
            _resident((len(POOL_WINDOWS), POOL_GROUP_DIM, POOL_GROUP_DIM)),
            _resident((1, POOL_WIDTH)),
            _resident((HGRN_WIDTH, D)),
            _resident((POOL_WIDTH, D)),
            _resident((D, D)),
            _resident((HEADS, HEAD_DIM, HEAD_DIM)),
            _resident((N_META, POOL_WIDTH)),
        ] + ffn_weight_specs,
        out_specs=[pl.BlockSpec((B, T, D), lambda t: (0, t, 0))] + ffn_weight_specs,
        out_shape=(jax.ShapeDtypeStruct((B, S, D), F32),
                   jax.ShapeDtypeStruct((D, d_ff), BF16),
                   jax.ShapeDtypeStruct((D, d_ff), BF16),
                   jax.ShapeDtypeStruct((d_ff, D), BF16)),
        scratch_shapes=[
            pltpu.VMEM((B * HEADS, HEAD_DIM, HEAD_DIM), F32),
            pltpu.VMEM((B, POOL_HISTORY, POOL_WIDTH), F32),
        ],
        compiler_params=pltpu.CompilerParams(
            dimension_semantics=("arbitrary",),
            vmem_limit_bytes=_vmem_limit(
                [((D, IN_WIDTH + HGRN_WIDTH + POOL_WIDTH + D), BF16),
                 (((B + 1) * HEADS, HEAD_DIM, HEAD_DIM), F32)],
                [((B, T, D), F32)] * 2
                + [(spec.block_shape, dtype) for spec in ffn_weight_specs
                   for dtype in (F32, BF16)])),
        name="mixer",
    )(x, g_mix, w_in_bf, lb_raw, hgrn_norm_g[0].reshape(1, HEAD_DIM),
      pool_w[0].astype(BF16), pool_scale[0].reshape(1, POOL_WIDTH),
      w_branch_a[0].astype(BF16), w_branch_b[0].astype(BF16), w_out[0].astype(BF16),
      state0, hist0, w_ffn_gate[0], w_ffn_up[0], w_ffn_down[0])

    TM = FFN_BLOCK
    out = pl.pallas_call(
        _ffn_kernel,
        grid=(B * S // TM,),
        in_specs=[
            pl.BlockSpec((TM, D), lambda i: (i, 0)),
            _resident((1, D)),
            _resident((D, d_ff)),
            _resident((D, d_ff)),
            _resident((d_ff, D)),
            _resident((1, D)),
        ],
        out_specs=pl.BlockSpec((TM, D), lambda i: (i, 0)),
        out_shape=jax.ShapeDtypeStruct((B * S, D), F32),
        compiler_params=pltpu.CompilerParams(
            dimension_semantics=("arbitrary",),
            vmem_limit_bytes=_vmem_limit([((D, 3 * d_ff), BF16)], [((TM, D), F32)] * 2)),
        name="ffn",
    )(h1.reshape(B * S, D), norm_ffn_g[0].reshape(1, D), w_gate_bf, w_up_bf, w_down_bf,
      norm_final_g.reshape(1, D))
    return out.reshape(B, S, D)
```

```python
import jax
import jax.numpy as jnp
from jax import lax
from jax.experimental import pallas as pl
from jax.experimental.pallas import tpu as pltpu

D_MODEL = 1024
N_META = 16
HEADS = 8
HEAD_DIM = 128
HGRN_WIDTH = HEADS * HEAD_DIM
POOL_WINDOWS = (2, 4, 8, 16)
POOL_GROUP_DIM = 128
POOL_WIDTH = len(POOL_WINDOWS) * POOL_GROUP_DIM
POOL_HISTORY = 16
EPS = 1e-6

OFF_Q = 0
OFF_F = OFF_Q + HGRN_WIDTH
OFF_I = OFF_F + HGRN_WIDTH
OFF_OG = OFF_I + HGRN_WIDTH
OFF_POOL = OFF_OG + HGRN_WIDTH
OFF_GA = OFF_POOL + POOL_WIDTH
OFF_GB = OFF_GA + D_MODEL
IN_WIDTH = OFF_GB + D_MODEL

SUBLANES = 8
BF16_ROWS = 2 * SUBLANES
MIXER_BLOCK = 256
SCORE_CHUNK = 128
DENSE_COLS = 256
MID_LEVEL = 8
FFN_BLOCK = 1024
FFN_ROWS = 256
VMEM_BYTES = 64 * 1024 * 1024
VMEM_TEMPORARIES_BYTES = 16 * 1024 * 1024

F32 = jnp.float32
BF16 = jnp.bfloat16


def _rmsnorm(x, g):
    ms = jnp.mean(x * x, axis=-1, keepdims=True)
    return x * lax.rsqrt(ms + EPS) * g


def _sigmoid(x):
    return 0.5 * jnp.tanh(0.5 * x) + 0.5


def _silu(x):
    h = 0.5 * x
    return h * jnp.tanh(h) + h


def _dot(a, b):
    return jnp.dot(a, b, preferred_element_type=F32)


def _dot_nt(a, b):
    return lax.dot_general(a, b, (((1,), (1,)), ((), ())), preferred_element_type=F32)


def _dot_tn(a, b):
    return lax.dot_general(a, b, (((0,), (0,)), ((), ())), preferred_element_type=F32)


def _lower_bound(lb_raw):
    r0, r1 = lb_raw[0:1, :], lb_raw[1:2, :]
    m = jnp.maximum(r0, r1)
    e0, e1 = jnp.exp(r0 - m), jnp.exp(r1 - m)
    return e0 / (e0 + e1)


def _cumsum_rows(g, tri):
    g1 = g.astype(BF16)
    rem = g - g1.astype(F32)
    g2 = rem.astype(BF16)
    g3 = (rem - g2.astype(F32)).astype(BF16)
    return _dot(tri, g1) + _dot(tri, g2) + _dot(tri, g3)


def _tri(n):
    r = lax.broadcasted_iota(jnp.int32, (n, n), 0)
    c = lax.broadcasted_iota(jnp.int32, (n, n), 1)
    return (c <= r).astype(BF16)


def _forget_gate(zf, lb):
    f = lb + (1.0 - lb) * _sigmoid(zf)
    return 1.0 - f, f, jnp.log2(f)


def _meta_state_kernel(meta_ref, g_ref, wf_ref, wi_ref, wp_ref, lb_raw_ref,
                       state_ref, hist_ref):
    n = _rmsnorm(meta_ref[...], g_ref[...]).astype(BF16)
    lb = _lower_bound(lb_raw_ref[...])
    k, _, g = _forget_gate(_dot(n, wf_ref[...]), lb)
    v = _dot(n, wi_ref[...])
    hist_ref[...] = _dot(n, wp_ref[...])
    b = _cumsum_rows(g, _tri(N_META))
    k_end = (k * jnp.exp2(b[N_META - 1:N_META, :] - b)).astype(BF16)
    v = v.astype(BF16)
    for h in range(HEADS):
        sl = slice(h * HEAD_DIM, (h + 1) * HEAD_DIM)
        state_ref[h] = _dot_tn(v[:, sl], k_end[:, sl])


def _sibling_boundary(b, m, sub):
    T = b.shape[0]
    if m >= SUBLANES:
        nb = T // (2 * m)
        r = b.reshape(nb, 2 * m, HEAD_DIM)[:, m - 1:m, :]
        return jnp.broadcast_to(r, (nb, 2 * m, HEAD_DIM)).reshape(T, HEAD_DIM)
    groups = b.reshape(T // SUBLANES, SUBLANES, HEAD_DIM)

    def bcast(i):
        r = jnp.broadcast_to(groups[:, i:i + 1, :], groups.shape)
        return r.reshape(T, HEAD_DIM)

    r = bcast(m - 1)
    for first in range(2 * m, SUBLANES, 2 * m):
        r = jnp.where(sub >= first, bcast(first + m - 1), r)
    return r


def _level_constants(T):
    row = lax.broadcasted_iota(jnp.int32, (T, HEAD_DIM), 0)
    r_i = lax.broadcasted_iota(jnp.int32, (SCORE_CHUNK, SCORE_CHUNK), 0)
    c_i = lax.broadcasted_iota(jnp.int32, (SCORE_CHUNK, SCORE_CHUNK), 1)
    consts = {}
    m = 1
    while m < T:
        later = ((row // m) % 2) == 1
        sign = jnp.where(later, 1.0, -1.0)
        mask = None
        if m < SCORE_CHUNK:
            mask = (((r_i // (2 * m)) == (c_i // (2 * m)))
                    & ((r_i // m) % 2 == 1) & ((c_i // m) % 2 == 0))
        consts[m] = (later, sign, mask)
        m *= 2
    return row % SUBLANES, consts


def _hgrn_scores(q, k, f, b, state_t, sub, consts, tick):
    T = q.shape[0]
    n_chunks = T // SCORE_CHUNK
    o_prev = _dot((q * jnp.exp2(b)).astype(BF16), state_t.T.astype(BF16))

    blocks = [[None] * n_chunks for _ in range(n_chunks)]
    m = 2
    while m < T:
        later, sign, mask = consts[m]
        r = _sibling_boundary(b, m, sub)
        z32 = jnp.where(later, q, k) * jnp.exp2((b - r) * sign)
        z = z32.astype(BF16)
        zc = [z[c * SCORE_CHUNK:(c + 1) * SCORE_CHUNK] for c in range(n_chunks)]
        zt = [z32[c * SCORE_CHUNK:(c + 1) * SCORE_CHUNK].T.astype(BF16)
              for c in range(n_chunks - (m >= SCORE_CHUNK))]
        if m < BF16_ROWS:
            for c in range(n_chunks):
                s = _dot(zc[c], zt[c])
                blocks[c][c] = jnp.where(mask, s, 0.0 if blocks[c][c] is None else blocks[c][c])
        elif m < SCORE_CHUNK:
            pairs = SCORE_CHUNK // (2 * m)
            for c in range(n_chunks):
                queries = jnp.concatenate(
                    [zc[c][(2 * j + 1) * m:(2 * j + 2) * m] for j in range(pairs)], axis=0)
                s = _dot(queries, zt[c])
                prev = blocks[c][c]
                rows = []
                for j in range(pairs):
                    lo, hi = (2 * j + 1) * m, (2 * j + 2) * m
                    rows.append(prev[lo - m:lo])
                    rows.append(jnp.where(mask[lo:hi], s[j * m:(j + 1) * m], prev[lo:hi]))
                blocks[c][c] = jnp.concatenate(rows, axis=0)
        else:
            span = m // SCORE_CHUNK
            for i in range(n_chunks):
                if (i // span) % 2 == 1:
                    for j in range((i // span - 1) * span, (i // span) * span):
                        blocks[i][j] = _dot(zc[i], zt[j])
        if m == MID_LEVEL:
            tick()
        m *= 2
    return o_prev, blocks


def _previous_row(x):
    T = x.shape[0]
    groups = x.reshape(T // SUBLANES, SUBLANES, HEAD_DIM)
    return pltpu.roll(groups, 1, 1).reshape(T, HEAD_DIM)


def _hgrn_output(o_prev, blocks, q, k, f, v, b, state_t, odd):
    T = q.shape[0]
    n_chunks = T // SCORE_CHUNK
    v_bf = v.astype(BF16)
    strips = []
    for i in range(n_chunks):
        a = jnp.concatenate(blocks[i][:i + 1], axis=1) if i else blocks[0][0]
        strips.append(_dot(a.astype(BF16), v_bf[:(i + 1) * SCORE_CHUNK]))
    o = o_prev + (jnp.concatenate(strips, axis=0) if n_chunks > 1 else strips[0])
    o = o + jnp.sum(q * k, axis=-1, keepdims=True) * v
    pair = jnp.where(odd, q * f, 0.0) * _previous_row(k)
    o = o + jnp.sum(pair, axis=-1, keepdims=True) * _previous_row(v)

    b_end = b[T - 1:T, :]
    k_end = (k * jnp.exp2(b_end - b)).astype(BF16)
    new_state_t = state_t * jnp.exp2(b_end) + _dot_tn(v_bf, k_end)
    return o, new_state_t


def _mixer_kernel(x_ref, g_mix_ref, w_in_ref, lb_raw_ref, g_hgrn_ref, pool_w_ref, pool_scale_ref,
                  w_a_ref, w_b_ref, w_out_ref, state0_ref, hist0_ref,
                  ffn_gate_ref, ffn_up_ref, ffn_down_ref,
                  out_ref, ffn_gate_bf_ref, ffn_up_bf_ref, ffn_down_bf_ref,
                  state_ref, hist_ref):
    T = MIXER_BLOCK
    B = x_ref.shape[0]

    @pl.when(pl.program_id(0) == 0)
    def _():
        for b in range(B):
            for h in range(HEADS):
                state_ref[b * HEADS + h] = state0_ref[h]
            hist_ref[b] = hist0_ref[...]

    ffn_gate_bf_ref[...] = ffn_gate_ref[...].astype(BF16)
    ffn_up_bf_ref[...] = ffn_up_ref[...].astype(BF16)
    ffn_down_bf_ref[...] = ffn_down_ref[...].astype(BF16)

    xs = [x_ref[b] for b in range(B)]
    ns = [_rmsnorm(x, g_mix_ref[...]).astype(BF16) for x in xs]
    lb = _lower_bound(lb_raw_ref[...])
    tri = _tri(T)
    sub, consts = _level_constants(T)
    g_hgrn = g_hgrn_ref[...]
    PAIR = 2 * HEAD_DIM

    queue = []

    def tick(n=1):
        for _ in range(n):
            if queue:
                queue.pop(0)[1]()

    def flush(tag=None):
        while queue if tag is None else any(t == tag for t, _ in queue):
            queue.pop(0)[1]()

    def enqueue_dot(tag, lhs, rhs_ref, col0, n_cols):
        blocks = [None] * (n_cols // DENSE_COLS)
        for cb in range(len(blocks)):
            def piece(cb=cb):
                cols = slice(col0 + cb * DENSE_COLS, col0 + (cb + 1) * DENSE_COLS)
                blocks[cb] = _dot(lhs(), rhs_ref[:, cols])
            queue.append((tag, piece))
        return blocks

    def assemble(blocks):
        return jnp.concatenate(blocks, axis=1) if len(blocks) > 1 else blocks[0]

    def enqueue_projection(tag, b, offset, width):
        return enqueue_dot(tag, lambda: ns[b], w_in_ref, offset, width)

    def enqueue_pair(b, p):
        return tuple(enqueue_projection(("pair", b, p), b, off + p * PAIR, PAIR)
                     for off in (OFF_Q, OFF_F, OFF_I, OFF_OG))

    def hgrn_pair(b, p, zq, zf, zi, zog):
        k, f, g = _forget_gate(zf, lb[:, p * PAIR:(p + 1) * PAIR])
        bsum = _cumsum_rows(g, tri)
        tick(2)
        q = _silu(zq)
        og = _silu(zog)
        heads = [(b * HEADS + 2 * p + j, slice(j * HEAD_DIM, (j + 1) * HEAD_DIM))
                 for j in range(2)]
        scores = []
        for h, sl in heads:
            scores.append(_hgrn_scores(q[:, sl], k[:, sl], f[:, sl], bsum[:, sl], state_ref[h],
                                       sub, consts, tick))
            tick()
        tick(2)
        ys = []
        for (h, sl), (o_prev, blocks) in zip(heads, scores):
            o, new_state = _hgrn_output(o_prev, blocks, q[:, sl], k[:, sl], f[:, sl], zi[:, sl],
                                        bsum[:, sl], state_ref[h], consts[1][0])
            state_ref[h] = new_state
            ys.append((_rmsnorm(o, g_hgrn) * og[:, sl]).astype(BF16))
        return jnp.concatenate(ys, axis=1)

    def pool_branch(b, u):
        ext = jnp.concatenate([hist_ref[b], u], axis=0)
        hist_ref[b] = u[T - POOL_HISTORY:, :]
        acc = ext
        span = 1
        ys = []
        for gi, w in enumerate(POOL_WINDOWS):
            while span < w:
                acc = acc + pltpu.roll(acc, span, 0)
                span *= 2
            sl = slice(gi * POOL_GROUP_DIM, (gi + 1) * POOL_GROUP_DIM)
            pooled = acc[POOL_HISTORY:, sl] * (1.0 / w) - u[:, sl]
            y = _dot(pooled.astype(BF16), pool_w_ref[gi]) * pool_scale_ref[:, sl]
            ys.append(y.astype(BF16))
        return jnp.concatenate(ys, axis=1)

    n_blocks = D_MODEL // DENSE_COLS

    def enqueue_gated_b(b, z_pool, z_gb, z_ga):
        hold = {}
        queue.append((("side", b), lambda: hold.update(yb=pool_branch(b, assemble(z_pool)))))
        branch_b = enqueue_dot(("side", b), lambda: hold["yb"], w_b_ref, 0, D_MODEL)
        gate_a, gated_b = [], []

        def gates():
            gate_a.extend(_sigmoid(z_ga[cb]) for cb in range(n_blocks))
            gated_b.extend(_sigmoid(z_gb[cb]) * branch_b[cb] for cb in range(n_blocks))
        queue.append((("side", b), gates))
        return gate_a, gated_b

    def enqueue_outputs(b, ya, gate_a, gated_b):
        hold = {}
        queue.append((("out", b), lambda: hold.update(ya=jnp.concatenate(ya, axis=1))))
        branch_a = enqueue_dot(("out", b), lambda: hold["ya"], w_a_ref, 0, D_MODEL)

        def merge():
            merged = [gate_a[cb] * branch_a[cb] + gated_b[cb] for cb in range(n_blocks)]
            hold["merged"] = jnp.concatenate(merged, axis=1).astype(BF16)
        queue.append((("out", b), merge))
        for cb in range(n_blocks):
            def piece(cb=cb):
                cols = slice(cb * DENSE_COLS, (cb + 1) * DENSE_COLS)
                out_ref[b, :, cols] = xs[b][:, cols] + _dot(hold["merged"], w_out_ref[:, cols])
            queue.append((("out", b), piece))

    n_pairs = HEADS // 2
    z = {(0, 0): enqueue_pair(0, 0)}
    flush()
    others, sides = {}, {}
    for b in range(B):
        for p in range(1, n_pairs):
            z[(b, p)] = enqueue_pair(b, p)
        if b + 1 < B:
            z[(b + 1, 0)] = enqueue_pair(b + 1, 0)
        others[b] = tuple(enqueue_projection(("other", b), b, off, width) for off, width in
                          ((OFF_POOL, POOL_WIDTH), (OFF_GB, D_MODEL), (OFF_GA, D_MODEL)))
        if b + 1 < B:
            sides[b] = enqueue_gated_b(b, *others[b])
    for b in range(B):
        ya = []
        for p in range(n_pairs):
            flush(("pair", b, p))
            ya.append(hgrn_pair(b, p, *(assemble(blocks) for blocks in z[(b, p)])))
        if b not in sides:
            sides[b] = enqueue_gated_b(b, *others[b])
        enqueue_outputs(b, ya, *sides[b])
    flush()


def _ffn_kernel(h_ref, g_ffn_ref, w_gate_ref, w_up_ref, w_down_ref, g_final_ref, out_ref):
    groups = [slice(r * FFN_ROWS, (r + 1) * FFN_ROWS) for r in range(FFN_BLOCK // FFN_ROWS)]

    def normed(rows):
        return _rmsnorm(h_ref[rows, :], g_ffn_ref[...]).astype(BF16)

    def finish(rows, down):
        out_ref[rows, :] = _rmsnorm(h_ref[rows, :] + down, g_final_ref[...])

    def activation(n):
        d_ff = w_gate_ref.shape[1]
        acts = []
        for c0 in range(0, d_ff, DENSE_COLS):
            cols = slice(c0, c0 + DENSE_COLS)
            gate = _dot(n, w_gate_ref[:, cols])
            up = _dot(n, w_up_ref[:, cols])
            acts.append((_silu(gate) * up).astype(BF16))
        return jnp.concatenate(acts, axis=1)

    n = normed(groups[0])
    pending = None
    for i, rows in enumerate(groups):
        act = activation(n)
        if pending is not None:
            finish(*pending)
        if i + 1 < len(groups):
            n = normed(groups[i + 1])
        pending = (rows, _dot(act, w_down_ref[...]))
    finish(*pending)


def _resident(shape):
    zeros = (0,) * len(shape)
    return pl.BlockSpec(shape, lambda *_: zeros, pipeline_mode=pl.Buffered(1))


def _vmem_limit(resident, streamed):
    def nbytes(arrays):
        total = 0
        for shape, dtype in arrays:
            size = jnp.dtype(dtype).itemsize
            for dim in shape:
                size *= dim
            total += size
        return total

    limit = nbytes(resident) + 2 * nbytes(streamed) + VMEM_TEMPORARIES_BYTES
    assert limit <= VMEM_BYTES, "blocks and resident weights do not fit VMEM"
    return limit


def _row_slabs(rows, cols, steps):
    slab = next(r for r in range(BF16_ROWS, rows + 1, BF16_ROWS)
                if rows % r == 0 and rows // r <= steps)
    last = rows // slab - 1
    return pl.BlockSpec((slab, cols), lambda t: (jnp.minimum(t, last), 0))


def kernel(x, meta_tokens, norm_mix_g, w_in, lb_raw, hgrn_norm_g, pool_w, pool_scale,
           w_branch_a, w_branch_b, w_out, norm_ffn_g, w_ffn_gate, w_ffn_up, w_ffn_down,
           norm_final_g):
    B, S, D = x.shape
    assert D == D_MODEL and S % MIXER_BLOCK == 0 and (B * S) % FFN_BLOCK == 0
    assert w_in.shape == (1, D_MODEL, IN_WIDTH) and meta_tokens.shape == (N_META, D_MODEL)
    d_ff = w_ffn_gate.shape[-1]

    w_in_bf = w_in[0].astype(BF16)
    g_mix = norm_mix_g[0].reshape(1, D)
    lb_raw = lb_raw.astype(F32)

    def whole(shape):
        zeros = (0,) * len(shape)
        return pl.BlockSpec(shape, lambda i: zeros)

    def w_in_columns(offset, width):
        assert offset % width == 0
        return pl.BlockSpec((D, width), lambda i: (0, offset // width))

    state0, hist0 = pl.pallas_call(
        _meta_state_kernel,
        grid=(1,),
        in_specs=[whole((N_META, D)), whole((1, D)),
                  w_in_columns(OFF_F, HGRN_WIDTH), w_in_columns(OFF_I, HGRN_WIDTH),
                  w_in_columns(OFF_POOL, POOL_WIDTH), whole((2, HGRN_WIDTH))],
        out_specs=(whole((HEADS, HEAD_DIM, HEAD_DIM)), whole((N_META, POOL_WIDTH))),
        out_shape=(jax.ShapeDtypeStruct((HEADS, HEAD_DIM, HEAD_DIM), F32),
                   jax.ShapeDtypeStruct((N_META, POOL_WIDTH), F32)),
        compiler_params=pltpu.CompilerParams(vmem_limit_bytes=_vmem_limit(
            [], [((D, 2 * HGRN_WIDTH + POOL_WIDTH), BF16), ((N_META, D), F32)])),
        name="meta_state",
    )(meta_tokens, g_mix, w_in_bf, w_in_bf, w_in_bf, lb_raw)

    T = MIXER_BLOCK
    steps = S // T
    ffn_weight_specs = [_row_slabs(D, d_ff, steps), _row_slabs(D, d_ff, steps),
                        _row_slabs(d_ff, D, steps)]
    h1, w_gate_bf, w_up_bf, w_down_bf = pl.pallas_call(
        _mixer_kernel,
        grid=(steps,),
        in_specs=[
            pl.BlockSpec((B, T, D), lambda t: (0, t, 0)),
            _resident((1, D)),
            _resident((D, IN_WIDTH)),
            _resident((2, HGRN_WIDTH)),
            _resident((1, HEAD_DIM)),
            _resident((len(POOL_WINDOWS), POOL_GROUP_DIM, POOL_GROUP_DIM)),
            _resident((1, POOL_WIDTH)),
            _resident((HGRN_WIDTH, D)),
            _resident((POOL_WIDTH, D)),
            _resident((D, D)),
            _resident((HEADS, HEAD_DIM, HEAD_DIM)),
            _resident((N_META, POOL_WIDTH)),
        ] + ffn_weight_specs,
        out_specs=[pl.BlockSpec((B, T, D), lambda t: (0, t, 0))] + ffn_weight_specs,
        out_shape=(jax.ShapeDtypeStruct((B, S, D), F32),
                   jax.ShapeDtypeStruct((D, d_ff), BF16),
                   jax.ShapeDtypeStruct((D, d_ff), BF16),
                   jax.ShapeDtypeStruct((d_ff, D), BF16)),
        scratch_shapes=[
            pltpu.VMEM((B * HEADS, HEAD_DIM, HEAD_DIM), F32),
            pltpu.VMEM((B, POOL_HISTORY, POOL_WIDTH), F32),
        ],
        compiler_params=pltpu.CompilerParams(
            dimension_semantics=("arbitrary",),
            vmem_limit_bytes=_vmem_limit(
                [((D, IN_WIDTH + HGRN_WIDTH + POOL_WIDTH + D), BF16),
                 (((B + 1) * HEADS, HEAD_DIM, HEAD_DIM), F32)],
                [((B, T, D), F32)] * 2
                + [(spec.block_shape, dtype) for spec in ffn_weight_specs
                   for dtype in (F32, BF16)])),
        name="mixer",
    )(x, g_mix, w_in_bf, lb_raw, hgrn_norm_g[0].reshape(1, HEAD_DIM),
      pool_w[0].astype(BF16), pool_scale[0].reshape(1, POOL_WIDTH),
      w_branch_a[0].astype(BF16), w_branch_b[0].astype(BF16), w_out[0].astype(BF16),
      state0, hist0, w_ffn_gate[0], w_ffn_up[0], w_ffn_down[0])

    TM = FFN_BLOCK
    out = pl.pallas_call(
        _ffn_kernel,
        grid=(B * S // TM,),
        in_specs=[
            pl.BlockSpec((TM, D), lambda i: (i, 0)),
            _resident((1, D)),
            _resident((D, d_ff)),
            _resident((D, d_ff)),
            _resident((d_ff, D)),
            _resident((1, D)),
        ],
        out_specs=pl.BlockSpec((TM, D), lambda i: (i, 0)),
        out_shape=jax.ShapeDtypeStruct((B * S, D), F32),
        compiler_params=pltpu.CompilerParams(
            dimension_semantics=("arbitrary",),
            vmem_limit_bytes=_vmem_limit([((D, 3 * d_ff), BF16)], [((TM, D), F32)] * 2)),
        name="ffn",
    )(h1.reshape(B * S, D), norm_ffn_g[0].reshape(1, D), w_gate_bf, w_up_bf, w_down_bf,
      norm_final_g.reshape(1, D))
    return out.reshape(B, S, D)
```

```python
import jax
import jax.numpy as jnp
from jax import lax
from jax.experimental import pallas as pl
from jax.experimental.pallas import tpu as pltpu

D_MODEL = 1024
N_META = 16
HEADS = 8
HEAD_DIM = 128
HGRN_WIDTH = HEADS * HEAD_DIM
POOL_WINDOWS = (2, 4, 8, 16)
POOL_GROUP_DIM = 128
POOL_WIDTH = len(POOL_WINDOWS) * POOL_GROUP_DIM
POOL_HISTORY = 16
EPS = 1e-6

OFF_Q = 0
OFF_F = OFF_Q + HGRN_WIDTH
OFF_I = OFF_F + HGRN_WIDTH
OFF_OG = OFF_I + HGRN_WIDTH
OFF_POOL = OFF_OG + HGRN_WIDTH
OFF_GA = OFF_POOL + POOL_WIDTH
OFF_GB = OFF_GA + D_MODEL
IN_WIDTH = OFF_GB + D_MODEL

SUBLANES = 8
BF16_ROWS = 2 * SUBLANES
MIXER_BLOCK = 256
SCORE_CHUNK = 128
DENSE_COLS = 256
MID_LEVEL = 8
FFN_BLOCK = 1024
FFN_ROWS = 256
VMEM_BYTES = 64 * 1024 * 1024
VMEM_TEMPORARIES_BYTES = 16 * 1024 * 1024

F32 = jnp.float32
BF16 = jnp.bfloat16


def _rmsnorm(x, g):
    ms = jnp.mean(x * x, axis=-1, keepdims=True)
    return x * lax.rsqrt(ms + EPS) * g


def _sigmoid(x):
    return 0.5 * jnp.tanh(0.5 * x) + 0.5


def _silu(x):
    h = 0.5 * x
    return h * jnp.tanh(h) + h


def _dot(a, b):
    return jnp.dot(a, b, preferred_element_type=F32)


def _dot_nt(a, b):
    return lax.dot_general(a, b, (((1,), (1,)), ((), ())), preferred_element_type=F32)


def _dot_tn(a, b):
    return lax.dot_general(a, b, (((0,), (0,)), ((), ())), preferred_element_type=F32)


def _lower_bound(lb_raw):
    r0, r1 = lb_raw[0:1, :], lb_raw[1:2, :]
    m = jnp.maximum(r0, r1)
    e0, e1 = jnp.exp(r0 - m), jnp.exp(r1 - m)
    return e0 / (e0 + e1)


def _cumsum_rows(g, tri):
    g1 = g.astype(BF16)
    rem = g - g1.astype(F32)
    g2 = rem.astype(BF16)
    g3 = (rem - g2.astype(F32)).astype(BF16)
    return _dot(tri, g1) + _dot(tri, g2) + _dot(tri, g3)


def _tri(n):
    r = lax.broadcasted_iota(jnp.int32, (n, n), 0)
    c = lax.broadcasted_iota(jnp.int32, (n, n), 1)
    return (c <= r).astype(BF16)


def _forget_gate(zf, lb):
    f = lb + (1.0 - lb) * _sigmoid(zf)
    return 1.0 - f, f, jnp.log2(f)


def _meta_state_kernel(meta_ref, g_ref, wf_ref, wi_ref, wp_ref, lb_raw_ref,
                       state_ref, hist_ref):
    n = _rmsnorm(meta_ref[...], g_ref[...]).astype(BF16)
    lb = _lower_bound(lb_raw_ref[...])
    k, _, g = _forget_gate(_dot(n, wf_ref[...]), lb)
    v = _dot(n, wi_ref[...])
    hist_ref[...] = _dot(n, wp_ref[...])
    b = _cumsum_rows(g, _tri(N_META))
    k_end = (k * jnp.exp2(b[N_META - 1:N_META, :] - b)).astype(BF16)
    v = v.astype(BF16)
    for h in range(HEADS):
        sl = slice(h * HEAD_DIM, (h + 1) * HEAD_DIM)
        state_ref[h] = _dot_tn(v[:, sl], k_end[:, sl])


def _sibling_boundary(b, m, sub):
    T = b.shape[0]
    if m >= SUBLANES:
        nb = T // (2 * m)
        r = b.reshape(nb, 2 * m, HEAD_DIM)[:, m - 1:m, :]
        return jnp.broadcast_to(r, (nb, 2 * m, HEAD_DIM)).reshape(T, HEAD_DIM)
    groups = b.reshape(T // SUBLANES, SUBLANES, HEAD_DIM)

    def bcast(i):
        r = jnp.broadcast_to(groups[:, i:i + 1, :], groups.shape)
        return r.reshape(T, HEAD_DIM)

    r = bcast(m - 1)
    for first in range(2 * m, SUBLANES, 2 * m):
        r = jnp.where(sub >= first, bcast(first + m - 1), r)
    return r


def _level_constants(T):
    row = lax.broadcasted_iota(jnp.int32, (T, HEAD_DIM), 0)
    r_i = lax.broadcasted_iota(jnp.int32, (SCORE_CHUNK, SCORE_CHUNK), 0)
    c_i = lax.broadcasted_iota(jnp.int32, (SCORE_CHUNK, SCORE_CHUNK), 1)
    consts = {}
    m = 1
    while m < T:
        later = ((row // m) % 2) == 1
        sign = jnp.where(later, 1.0, -1.0)
        mask = None
        if m < SCORE_CHUNK:
            mask = (((r_i // (2 * m)) == (c_i // (2 * m)))
                    & ((r_i // m) % 2 == 1) & ((c_i // m) % 2 == 0))
        consts[m] = (later, sign, mask)
        m *= 2
    return row % SUBLANES, consts


def _hgrn_scores(q, k, f, b, state_t, sub, consts, tick):
    T = q.shape[0]
    n_chunks = T // SCORE_CHUNK
    o_prev = _dot((q * jnp.exp2(b)).astype(BF16), state_t.T.astype(BF16))

    blocks = [[None] * n_chunks for _ in range(n_chunks)]
    m = 2
    while m < T:
        later, sign, mask = consts[m]
        if m < SUBLANES:
            r = _sibling_boundary(b, m, sub)
            z32 = jnp.where(later, q, k) * jnp.exp2((b - r) * sign)
            z = z32.astype(BF16)
            for c in range(n_chunks):
                rows = slice(c * SCORE_CHUNK, (c + 1) * SCORE_CHUNK)
                s = _dot(z[rows], z32[rows].T.astype(BF16))
                blocks[c][c] = jnp.where(mask, s, 0.0 if blocks[c][c] is None else blocks[c][c])
        else:
            n_pairs = T // (2 * m)
            split = (n_pairs, 2, m, HEAD_DIM)
            b4 = b.reshape(split)
            r = b4[:, 0, m - 1:m, :]
            keys = k.reshape(split)[:, 0] * jnp.exp2(r - b4[:, 0])
            queries = q.reshape(split)[:, 1] * jnp.exp2(b4[:, 1] - r)
            if m < SCORE_CHUNK:
                per_chunk = SCORE_CHUNK // (2 * m)
                for c in range(n_chunks):
                    mine = slice(c * per_chunk, (c + 1) * per_chunk)
                    in_place = jnp.stack([keys[mine], queries[mine]], axis=1)
                    zt = in_place.reshape(SCORE_CHUNK, HEAD_DIM).T.astype(BF16)
                    lhs = queries[mine].reshape(per_chunk * m, HEAD_DIM).astype(BF16)
                    s = _dot(lhs, zt)
                    prev = blocks[c][c]
                    rows = []
                    for j in range(per_chunk):
                        lo, hi = (2 * j + 1) * m, (2 * j + 2) * m
                        rows.append(prev[lo - m:lo])
                        rows.append(jnp.where(mask[lo:hi], s[j * m:(j + 1) * m], prev[lo:hi]))
                    blocks[c][c] = jnp.concatenate(rows, axis=0)
            else:
                span = m // SCORE_CHUNK
                for pair in range(n_pairs):
                    for i in range(span):
                        lhs = queries[pair, i * SCORE_CHUNK:(i + 1) * SCORE_CHUNK].astype(BF16)
                        for j in range(span):
                            zt = keys[pair, j * SCORE_CHUNK:(j + 1) * SCORE_CHUNK].T.astype(BF16)
                            blocks[(2 * pair + 1) * span + i][2 * pair * span + j] = _dot(lhs, zt)
        if m == MID_LEVEL:
            tick()
        m *= 2
    return o_prev, blocks


def _previous_row(x):
    T = x.shape[0]
    groups = x.reshape(T // SUBLANES, SUBLANES, HEAD_DIM)
    return pltpu.roll(groups, 1, 1).reshape(T, HEAD_DIM)


def _hgrn_output(o_prev, blocks, q, k, f, v, b, state_t, odd):
    T = q.shape[0]
    n_chunks = T // SCORE_CHUNK
    v_bf = v.astype(BF16)
    strips = []
    for i in range(n_chunks):
        a = jnp.concatenate(blocks[i][:i + 1], axis=1) if i else blocks[0][0]
        strips.append(_dot(a.astype(BF16), v_bf[:(i + 1) * SCORE_CHUNK]))
    o = o_prev + (jnp.concatenate(strips, axis=0) if n_chunks > 1 else strips[0])
    o = o + jnp.sum(q * k, axis=-1, keepdims=True) * v
    pair = jnp.where(odd, q * f, 0.0) * _previous_row(k)
    o = o + jnp.sum(pair, axis=-1, keepdims=True) * _previous_row(v)

    b_end = b[T - 1:T, :]
    k_end = (k * jnp.exp2(b_end - b)).astype(BF16)
    new_state_t = state_t * jnp.exp2(b_end) + _dot_tn(v_bf, k_end)
    return o, new_state_t


def _mixer_kernel(x_ref, g_mix_ref, w_in_ref, lb_raw_ref, g_hgrn_ref, pool_w_ref, pool_scale_ref,
                  w_a_ref, w_b_ref, w_out_ref, state0_ref, hist0_ref,
                  ffn_gate_ref, ffn_up_ref, ffn_down_ref,
                  out_ref, ffn_gate_bf_ref, ffn_up_bf_ref, ffn_down_bf_ref,
                  state_ref, hist_ref):
    T = MIXER_BLOCK
    B = x_ref.shape[0]

    @pl.when(pl.program_id(0) == 0)
    def _():
        for b in range(B):
            for h in range(HEADS):
                state_ref[b * HEADS + h] = state0_ref[h]
            hist_ref[b] = hist0_ref[...]

    ffn_gate_bf_ref[...] = ffn_gate_ref[...].astype(BF16)
    ffn_up_bf_ref[...] = ffn_up_ref[...].astype(BF16)
    ffn_down_bf_ref[...] = ffn_down_ref[...].astype(BF16)

    xs = [x_ref[b] for b in range(B)]
    ns = [_rmsnorm(x, g_mix_ref[...]).astype(BF16) for x in xs]
    lb = _lower_bound(lb_raw_ref[...])
    tri = _tri(T)
    sub, consts = _level_constants(T)
    g_hgrn = g_hgrn_ref[...]
    PAIR = 2 * HEAD_DIM

    queue = []

    def tick(n=1):
        for _ in range(n):
            if queue:
                queue.pop(0)[1]()

    def flush(tag=None):
        while queue if tag is None else any(t == tag for t, _ in queue):
            queue.pop(0)[1]()

    def enqueue_dot(tag, lhs, rhs_ref, col0, n_cols):
        blocks = [None] * (n_cols // DENSE_COLS)
        for cb in range(len(blocks)):
            def piece(cb=cb):
                cols = slice(col0 + cb * DENSE_COLS, col0 + (cb + 1) * DENSE_COLS)
                blocks[cb] = _dot(lhs(), rhs_ref[:, cols])
            queue.append((tag, piece))
        return blocks

    def assemble(blocks):
        return jnp.concatenate(blocks, axis=1) if len(blocks) > 1 else blocks[0]

    def enqueue_projection(tag, b, offset, width):
        return enqueue_dot(tag, lambda: ns[b], w_in_ref, offset, width)

    def enqueue_pair(b, p):
        return tuple(enqueue_projection(("pair", b, p), b, off + p * PAIR, PAIR)
                     for off in (OFF_Q, OFF_F, OFF_I, OFF_OG))

    def hgrn_pair(b, p, zq, zf, zi, zog):
        k, f, g = _forget_gate(zf, lb[:, p * PAIR:(p + 1) * PAIR])
        bsum = _cumsum_rows(g, tri)
        tick(2)
        q = _silu(zq)
        og = _silu(zog)
        heads = [(b * HEADS + 2 * p + j, slice(j * HEAD_DIM, (j + 1) * HEAD_DIM))
                 for j in range(2)]
        scores = []
        for h, sl in heads:
            scores.append(_hgrn_scores(q[:, sl], k[:, sl], f[:, sl], bsum[:, sl], state_ref[h],
                                       sub, consts, tick))
            tick()
        tick(2)
        ys = []
        for (h, sl), (o_prev, blocks) in zip(heads, scores):
            o, new_state = _hgrn_output(o_prev, blocks, q[:, sl], k[:, sl], f[:, sl], zi[:, sl],
                                        bsum[:, sl], state_ref[h], consts[1][0])
            state_ref[h] = new_state
            ys.append((_rmsnorm(o, g_hgrn) * og[:, sl]).astype(BF16))
        return jnp.concatenate(ys, axis=1)

    def pool_branch(b, u):
        ext = jnp.concatenate([hist_ref[b], u], axis=0)
        hist_ref[b] = u[T - POOL_HISTORY:, :]
        acc = ext
        span = 1
        ys = []
        for gi, w in enumerate(POOL_WINDOWS):
            while span < w:
                acc = acc + pltpu.roll(acc, span, 0)
                span *= 2
            sl = slice(gi * POOL_GROUP_DIM, (gi + 1) * POOL_GROUP_DIM)
            pooled = acc[POOL_HISTORY:, sl] * (1.0 / w) - u[:, sl]
            y = _dot(pooled.astype(BF16), pool_w_ref[gi]) * pool_scale_ref[:, sl]
            ys.append(y.astype(BF16))
        return jnp.concatenate(ys, axis=1)

    n_blocks = D_MODEL // DENSE_COLS

    def enqueue_gated_b(b, z_pool, z_gb, z_ga):
        hold = {}
        queue.append((("side", b), lambda: hold.update(yb=pool_branch(b, assemble(z_pool)))))
        branch_b = enqueue_dot(("side", b), lambda: hold["yb"], w_b_ref, 0, D_MODEL)
        gate_a, gated_b = [], []

        def gates():
            gate_a.extend(_sigmoid(z_ga[cb]) for cb in range(n_blocks))
            gated_b.extend(_sigmoid(z_gb[cb]) * branch_b[cb] for cb in range(n_blocks))
        queue.append((("side", b), gates))
        return gate_a, gated_b

    def enqueue_outputs(b, ya, gate_a, gated_b):
        hold = {}
        queue.append((("out", b), lambda: hold.update(ya=jnp.concatenate(ya, axis=1))))
        branch_a = enqueue_dot(("out", b), lambda: hold["ya"], w_a_ref, 0, D_MODEL)

        def merge():
            merged = [gate_a[cb] * branch_a[cb] + gated_b[cb] for cb in range(n_blocks)]
            hold["merged"] = jnp.concatenate(merged, axis=1).astype(BF16)
        queue.append((("out", b), merge))
        for cb in range(n_blocks):
            def piece(cb=cb):
                cols = slice(cb * DENSE_COLS, (cb + 1) * DENSE_COLS)
                out_ref[b, :, cols] = xs[b][:, cols] + _dot(hold["merged"], w_out_ref[:, cols])
            queue.append((("out", b), piece))

    n_pairs = HEADS // 2
    z = {(0, 0): enqueue_pair(0, 0)}
    flush()
    others, sides = {}, {}
    for b in range(B):
        for p in range(1, n_pairs):
            z[(b, p)] = enqueue_pair(b, p)
        if b + 1 < B:
            z[(b + 1, 0)] = enqueue_pair(b + 1, 0)
        others[b] = tuple(enqueue_projection(("other", b), b, off, width) for off, width in
                          ((OFF_POOL, POOL_WIDTH), (OFF_GB, D_MODEL), (OFF_GA, D_MODEL)))
        if b + 1 < B:
            sides[b] = enqueue_gated_b(b, *others[b])
    for b in range(B):
        ya = []
        for p in range(n_pairs):
            flush(("pair", b, p))
            ya.append(hgrn_pair(b, p, *(assemble(blocks) for blocks in z[(b, p)])))
        if b not in sides:
            sides[b] = enqueue_gated_b(b, *others[b])
        enqueue_outputs(b, ya, *sides[b])
    flush()


def _ffn_kernel(h_ref, g_ffn_ref, w_gate_ref, w_up_ref, w_down_ref, g_final_ref, out_ref):
    groups = [slice(r * FFN_ROWS, (r + 1) * FFN_ROWS) for r in range(FFN_BLOCK // FFN_ROWS)]

    def normed(rows):
        return _rmsnorm(h_ref[rows, :], g_ffn_ref[...]).astype(BF16)

    def finish(rows, down):
        out_ref[rows, :] = _rmsnorm(h_ref[rows, :] + down, g_final_ref[...])

    def activation(n):
        d_ff = w_gate_ref.shape[1]
        acts = []
        for c0 in range(0, d_ff, DENSE_COLS):
            cols = slice(c0, c0 + DENSE_COLS)
            gate = _dot(n, w_gate_ref[:, cols])
            up = _dot(n, w_up_ref[:, cols])
            acts.append((_silu(gate) * up).astype(BF16))
        return jnp.concatenate(acts, axis=1)

    n = normed(groups[0])
    pending = None
    for i, rows in enumerate(groups):
        act = activation(n)
        if pending is not None:
            finish(*pending)
        if i + 1 < len(groups):
            n = normed(groups[i + 1])
        pending = (rows, _dot(act, w_down_ref[...]))
    finish(*pending)


def _resident(shape):
    zeros = (0,) * len(shape)
    return pl.BlockSpec(shape, lambda *_: zeros, pipeline_mode=pl.Buffered(1))


def _vmem_limit(resident, streamed):
    def nbytes(arrays):
        total = 0
        for shape, dtype in arrays:
            size = jnp.dtype(dtype).itemsize
            for dim in shape:
                size *= dim
            total += size
        return total

    limit = nbytes(resident) + 2 * nbytes(streamed) + VMEM_TEMPORARIES_BYTES
    assert limit <= VMEM_BYTES, "blocks and resident weights do not fit VMEM"
    return limit


def _row_slabs(rows, cols, steps):
    slab = next(r for r in range(BF16_ROWS, rows + 1, BF16_ROWS)
                if rows % r == 0 and rows // r <= steps)
    last = rows // slab - 1
    return pl.BlockSpec((slab, cols), lambda t: (jnp.minimum(t, last), 0))


def kernel(x, meta_tokens, norm_mix_g, w_in, lb_raw, hgrn_norm_g, pool_w, pool_scale,
           w_branch_a, w_branch_b, w_out, norm_ffn_g, w_ffn_gate, w_ffn_up, w_ffn_down,
           norm_final_g):
    B, S, D = x.shape
    assert D == D_MODEL and S % MIXER_BLOCK == 0 and (B * S) % FFN_BLOCK == 0
    assert w_in.shape == (1, D_MODEL, IN_WIDTH) and meta_tokens.shape == (N_META, D_MODEL)
    d_ff = w_ffn_gate.shape[-1]

    w_in_bf = w_in[0].astype(BF16)
    g_mix = norm_mix_g[0].reshape(1, D)
    lb_raw = lb_raw.astype(F32)

    def whole(shape):
        zeros = (0,) * len(shape)
        return pl.BlockSpec(shape, lambda i: zeros)

    def w_in_columns(offset, width):
        assert offset % width == 0
        return pl.BlockSpec((D, width), lambda i: (0, offset // width))

    state0, hist0 = pl.pallas_call(
        _meta_state_kernel,
        grid=(1,),
        in_specs=[whole((N_META, D)), whole((1, D)),
                  w_in_columns(OFF_F, HGRN_WIDTH), w_in_columns(OFF_I, HGRN_WIDTH),
                  w_in_columns(OFF_POOL, POOL_WIDTH), whole((2, HGRN_WIDTH))],
        out_specs=(whole((HEADS, HEAD_DIM, HEAD_DIM)), whole((N_META, POOL_WIDTH))),
        out_shape=(jax.ShapeDtypeStruct((HEADS, HEAD_DIM, HEAD_DIM), F32),
                   jax.ShapeDtypeStruct((N_META, POOL_WIDTH), F32)),
        compiler_params=pltpu.CompilerParams(vmem_limit_bytes=_vmem_limit(
            [], [((D, 2 * HGRN_WIDTH + POOL_WIDTH), BF16), ((N_META, D), F32)])),
        name="meta_state",
    )(meta_tokens, g_mix, w_in_bf, w_in_bf, w_in_bf, lb_raw)

    T = MIXER_BLOCK
    steps = S // T
    ffn_weight_specs = [_row_slabs(D, d_ff, steps), _row_slabs(D, d_ff, steps),
                        _row_slabs(d_ff, D, steps)]
    h1, w_gate_bf, w_up_bf, w_down_bf = pl.pallas_call(
        _mixer_kernel,
        grid=(steps,),
        in_specs=[
            pl.BlockSpec((B, T, D), lambda t: (0, t, 0)),
            _resident((1, D)),
            _resident((D, IN_WIDTH)),
            _resident((2, HGRN_WIDTH)),
            _resident((1, HEAD_DIM)),
            _resident((len(POOL_WINDOWS), POOL_GROUP_DIM, POOL_GROUP_DIM)),
            _resident((1, POOL_WIDTH)),
            _resident((HGRN_WIDTH, D)),
            _resident((POOL_WIDTH, D)),
            _resident((D, D)),
            _resident((HEADS, HEAD_DIM, HEAD_DIM)),
            _resident((N_META, POOL_WIDTH)),
        ] + ffn_weight_specs,
        out_specs=[pl.BlockSpec((B, T, D), lambda t: (0, t, 0))] + ffn_weight_specs,
        out_shape=(jax.ShapeDtypeStruct((B, S, D), F32),
                   jax.ShapeDtypeStruct((D, d_ff), BF16),
                   jax.ShapeDtypeStruct((D, d_ff), BF16),
                   jax.ShapeDtypeStruct((d_ff, D), BF16)),
        scratch_shapes=[
            pltpu.VMEM((B * HEADS, HEAD_DIM, HEAD_DIM), F32),
            pltpu.VMEM((B, POOL_HISTORY, POOL_WIDTH), F32),
        ],
        compiler_params=pltpu.CompilerParams(
            dimension_semantics=("arbitrary",),
            vmem_limit_bytes=_vmem_limit(
                [((D, IN_WIDTH + HGRN_WIDTH + POOL_WIDTH + D), BF16),
                 (((B + 1) * HEADS, HEAD_DIM, HEAD_DIM), F32)],
                [((B, T, D), F32)] * 2
                + [(spec.block_shape, dtype) for spec in ffn_weight_specs
                   for dtype in (F32, BF16)])),
        name="mixer",
    )(x, g_mix, w_in_bf, lb_raw, hgrn_norm_g[0].reshape(1, HEAD_DIM),
      pool_w[0].astype(BF16), pool_scale[0].reshape(1, POOL_WIDTH),
      w_branch_a[0].astype(BF16), w_branch_b[0].astype(BF16), w_out[0].astype(BF16),
      state0, hist0, w_ffn_gate[0], w_ffn_up[0], w_ffn_down[0])

    TM = FFN_BLOCK
    out = pl.pallas_call(
        _ffn_kernel,
        grid=(B * S // TM,),
        in_specs=[
            pl.BlockSpec((TM, D), lambda i: (i, 0)),
            _resident((1, D)),
            _resident((D, d_ff)),
            _resident((D, d_ff)),
            _resident((d_ff, D)),
            _resident((1, D)),
        ],
        out_specs=pl.BlockSpec((TM, D), lambda i: (i, 0)),
        out_shape=jax.ShapeDtypeStruct((B * S, D), F32),
        compiler_params=pltpu.CompilerParams(
            dimension_semantics=("arbitrary",),
            vmem_limit_bytes=_vmem_limit([((D, 3 * d_ff), BF16)], [((TM, D), F32)] * 2)),
        name="ffn",
    )(h1.reshape(B * S, D), norm_ffn_g[0].reshape(1, D), w_gate_bf, w_up_bf, w_down_bf,
      norm_final_g.reshape(1, D))
    return out.reshape(B, S, D)
```

```python
import jax
import jax.numpy as jnp
from jax import lax
from jax.experimental import pallas as pl
from jax.experimental.pallas import tpu as pltpu

D_MODEL = 1024
N_META = 16
HEADS = 8
HEAD_DIM = 128
HGRN_WIDTH = HEADS * HEAD_DIM
POOL_WINDOWS = (2, 4, 8, 16)
POOL_GROUP_DIM = 128
POOL_WIDTH = len(POOL_WINDOWS) * POOL_GROUP_DIM
POOL_HISTORY = 16
EPS = 1e-6

OFF_Q = 0
OFF_F = OFF_Q + HGRN_WIDTH
OFF_I = OFF_F + HGRN_WIDTH
OFF_OG = OFF_I + HGRN_WIDTH
OFF_POOL = OFF_OG + HGRN_WIDTH
OFF_GA = OFF_POOL + POOL_WIDTH
OFF_GB = OFF_GA + D_MODEL
IN_WIDTH = OFF_GB + D_MODEL

SUBLANES = 8
BF16_ROWS = 2 * SUBLANES
MIXER_BLOCK = 256
SCORE_CHUNK = 128
DENSE_COLS = 256
MID_LEVEL = 8
FFN_BLOCK = 1024
FFN_ROWS = 256
VMEM_BYTES = 64 * 1024 * 1024
VMEM_TEMPORARIES_BYTES = 16 * 1024 * 1024

F32 = jnp.float32
BF16 = jnp.bfloat16


def _rmsnorm(x, g):
    ms = jnp.mean(x * x, axis=-1, keepdims=True)
    return x * lax.rsqrt(ms + EPS) * g


def _sigmoid(x):
    return 0.5 * jnp.tanh(0.5 * x) + 0.5


def _silu(x):
    h = 0.5 * x
    return h * jnp.tanh(h) + h


def _dot(a, b):
    return jnp.dot(a, b, preferred_element_type=F32)


def _dot_nt(a, b):
    return lax.dot_general(a, b, (((1,), (1,)), ((), ())), preferred_element_type=F32)


def _dot_tn(a, b):
    return lax.dot_general(a, b, (((0,), (0,)), ((), ())), preferred_element_type=F32)


def _lower_bound(lb_raw):
    r0, r1 = lb_raw[0:1, :], lb_raw[1:2, :]
    m = jnp.maximum(r0, r1)
    e0, e1 = jnp.exp(r0 - m), jnp.exp(r1 - m)
    return e0 / (e0 + e1)


def _cumsum_rows(g, tri):
    g1 = g.astype(BF16)
    rem = g - g1.astype(F32)
    g2 = rem.astype(BF16)
    g3 = (rem - g2.astype(F32)).astype(BF16)
    return _dot(tri, g1) + _dot(tri, g2) + _dot(tri, g3)


def _tri(n):
    r = lax.broadcasted_iota(jnp.int32, (n, n), 0)
    c = lax.broadcasted_iota(jnp.int32, (n, n), 1)
    return (c <= r).astype(BF16)


def _forget_gate(zf, lb):
    f = lb + (1.0 - lb) * _sigmoid(zf)
    return 1.0 - f, f, jnp.log2(f)


def _meta_state(meta, g_mix, w_in_ref, lb_raw):
    n = _rmsnorm(meta, g_mix).astype(BF16)
    k, _, g = _forget_gate(_dot(n, w_in_ref[:, OFF_F:OFF_F + HGRN_WIDTH]), _lower_bound(lb_raw))
    v = _dot(n, w_in_ref[:, OFF_I:OFF_I + HGRN_WIDTH]).astype(BF16)
    history = _dot(n, w_in_ref[:, OFF_POOL:OFF_POOL + POOL_WIDTH])
    b = _cumsum_rows(g, _tri(N_META))
    k_end = (k * jnp.exp2(b[N_META - 1:N_META, :] - b)).astype(BF16)
    states = [_dot_tn(v[:, h * HEAD_DIM:(h + 1) * HEAD_DIM],
                      k_end[:, h * HEAD_DIM:(h + 1) * HEAD_DIM]) for h in range(HEADS)]
    return states, history


def _sibling_boundary(b, m, sub):
    T = b.shape[0]
    if m >= SUBLANES:
        nb = T // (2 * m)
        r = b.reshape(nb, 2 * m, HEAD_DIM)[:, m - 1:m, :]
        return jnp.broadcast_to(r, (nb, 2 * m, HEAD_DIM)).reshape(T, HEAD_DIM)
    groups = b.reshape(T // SUBLANES, SUBLANES, HEAD_DIM)

    def bcast(i):
        r = jnp.broadcast_to(groups[:, i:i + 1, :], groups.shape)
        return r.reshape(T, HEAD_DIM)

    r = bcast(m - 1)
    for first in range(2 * m, SUBLANES, 2 * m):
        r = jnp.where(sub >= first, bcast(first + m - 1), r)
    return r


def _level_constants(T):
    row = lax.broadcasted_iota(jnp.int32, (T, HEAD_DIM), 0)
    r_i = lax.broadcasted_iota(jnp.int32, (SCORE_CHUNK, SCORE_CHUNK), 0)
    c_i = lax.broadcasted_iota(jnp.int32, (SCORE_CHUNK, SCORE_CHUNK), 1)
    consts = {}
    m = 1
    while m < T:
        later = ((row // m) % 2) == 1
        sign = jnp.where(later, 1.0, -1.0)
        mask = None
        if m < SCORE_CHUNK:
            mask = (((r_i // (2 * m)) == (c_i // (2 * m)))
                    & ((r_i // m) % 2 == 1) & ((c_i // m) % 2 == 0))
        consts[m] = (later, sign, mask)
        m *= 2
    return row % SUBLANES, consts


def _hgrn_scores(q, k, f, b, state_t, sub, consts, tick):
    T = q.shape[0]
    n_chunks = T // SCORE_CHUNK
    o_prev = _dot((q * jnp.exp2(b)).astype(BF16), state_t.T.astype(BF16))

    blocks = [[None] * n_chunks for _ in range(n_chunks)]
    m = 2
    while m < T:
        later, sign, mask = consts[m]
        r = _sibling_boundary(b, m, sub)
        z32 = jnp.where(later, q, k) * jnp.exp2((b - r) * sign)
        z = z32.astype(BF16)
        zc = [z[c * SCORE_CHUNK:(c + 1) * SCORE_CHUNK] for c in range(n_chunks)]
        zt = [z32[c * SCORE_CHUNK:(c + 1) * SCORE_CHUNK].T.astype(BF16)
              for c in range(n_chunks - (m >= SCORE_CHUNK))]
        if m < BF16_ROWS:
            for c in range(n_chunks):
                s = _dot(zc[c], zt[c])
                blocks[c][c] = jnp.where(mask, s, 0.0 if blocks[c][c] is None else blocks[c][c])
        elif m < SCORE_CHUNK:
            pairs = SCORE_CHUNK // (2 * m)
            for c in range(n_chunks):
                queries = jnp.concatenate(
                    [zc[c][(2 * j + 1) * m:(2 * j + 2) * m] for j in range(pairs)], axis=0)
                s = _dot(queries, zt[c])
                prev = blocks[c][c]
                rows = []
                for j in range(pairs):
                    lo, hi = (2 * j + 1) * m, (2 * j + 2) * m
                    rows.append(prev[lo - m:lo])
                    rows.append(jnp.where(mask[lo:hi], s[j * m:(j + 1) * m], prev[lo:hi]))
                blocks[c][c] = jnp.concatenate(rows, axis=0)
        else:
            span = m // SCORE_CHUNK
            for i in range(n_chunks):
                if (i // span) % 2 == 1:
                    for j in range((i // span - 1) * span, (i // span) * span):
                        blocks[i][j] = _dot(zc[i], zt[j])
        if m == MID_LEVEL:
            tick()
        m *= 2
    return o_prev, blocks


def _previous_row(x):
    T = x.shape[0]
    groups = x.reshape(T // SUBLANES, SUBLANES, HEAD_DIM)
    return pltpu.roll(groups, 1, 1).reshape(T, HEAD_DIM)


def _hgrn_output(o_prev, blocks, q, k, f, v, b, state_t, odd):
    T = q.shape[0]
    n_chunks = T // SCORE_CHUNK
    v_bf = v.astype(BF16)
    strips = []
    for i in range(n_chunks):
        a = jnp.concatenate(blocks[i][:i + 1], axis=1) if i else blocks[0][0]
        strips.append(_dot(a.astype(BF16), v_bf[:(i + 1) * SCORE_CHUNK]))
    o = o_prev + (jnp.concatenate(strips, axis=0) if n_chunks > 1 else strips[0])
    o = o + jnp.sum(q * k, axis=-1, keepdims=True) * v
    pair = jnp.where(odd, q * f, 0.0) * _previous_row(k)
    o = o + jnp.sum(pair, axis=-1, keepdims=True) * _previous_row(v)

    b_end = b[T - 1:T, :]
    k_end = (k * jnp.exp2(b_end - b)).astype(BF16)
    new_state_t = state_t * jnp.exp2(b_end) + _dot_tn(v_bf, k_end)
    return o, new_state_t


def _mixer_kernel(x_ref, g_mix_ref, w_in_ref, lb_raw_ref, g_hgrn_ref, pool_w_ref, pool_scale_ref,
                  w_a_ref, w_b_ref, w_out_ref, meta_ref,
                  ffn_gate_ref, ffn_up_ref, ffn_down_ref,
                  out_ref, ffn_gate_bf_ref, ffn_up_bf_ref, ffn_down_bf_ref,
                  state_ref, hist_ref):
    T = MIXER_BLOCK
    B = x_ref.shape[0]

    @pl.when(pl.program_id(0) == 0)
    def _():
        states, history = _meta_state(meta_ref[...], g_mix_ref[...], w_in_ref, lb_raw_ref[...])
        for b in range(B):
            for h in range(HEADS):
                state_ref[b * HEADS + h] = states[h]
            hist_ref[b] = history

    ffn_gate_bf_ref[...] = ffn_gate_ref[...].astype(BF16)
    ffn_up_bf_ref[...] = ffn_up_ref[...].astype(BF16)
    ffn_down_bf_ref[...] = ffn_down_ref[...].astype(BF16)

    xs = [x_ref[b] for b in range(B)]
    ns = [_rmsnorm(x, g_mix_ref[...]).astype(BF16) for x in xs]
    lb = _lower_bound(lb_raw_ref[...])
    tri = _tri(T)
    sub, consts = _level_constants(T)
    g_hgrn = g_hgrn_ref[...]
    PAIR = 2 * HEAD_DIM

    queue = []

    def tick(n=1):
        for _ in range(n):
            if queue:
                queue.pop(0)[1]()

    def flush(tag=None):
        while queue if tag is None else any(t == tag for t, _ in queue):
            queue.pop(0)[1]()

    def enqueue_dot(tag, lhs, rhs_ref, col0, n_cols):
        blocks = [None] * (n_cols // DENSE_COLS)
        for cb in range(len(blocks)):
            def piece(cb=cb):
                cols = slice(col0 + cb * DENSE_COLS, col0 + (cb + 1) * DENSE_COLS)
                blocks[cb] = _dot(lhs(), rhs_ref[:, cols])
            queue.append((tag, piece))
        return blocks

    def assemble(blocks):
        return jnp.concatenate(blocks, axis=1) if len(blocks) > 1 else blocks[0]

    def enqueue_projection(tag, b, offset, width):
        return enqueue_dot(tag, lambda: ns[b], w_in_ref, offset, width)

    def enqueue_pair(b, p):
        return tuple(enqueue_projection(("pair", b, p), b, off + p * PAIR, PAIR)
                     for off in (OFF_Q, OFF_F, OFF_I, OFF_OG))

    def hgrn_pair(b, p, zq, zf, zi, zog):
        k, f, g = _forget_gate(zf, lb[:, p * PAIR:(p + 1) * PAIR])
        bsum = _cumsum_rows(g, tri)
        tick(2)
        q = _silu(zq)
        og = _silu(zog)
        heads = [(b * HEADS + 2 * p + j, slice(j * HEAD_DIM, (j + 1) * HEAD_DIM))
                 for j in range(2)]
        scores = []
        for h, sl in heads:
            scores.append(_hgrn_scores(q[:, sl], k[:, sl], f[:, sl], bsum[:, sl], state_ref[h],
                                       sub, consts, tick))
            tick()
        tick(2)
        ys = []
        for (h, sl), (o_prev, blocks) in zip(heads, scores):
            o, new_state = _hgrn_output(o_prev, blocks, q[:, sl], k[:, sl], f[:, sl], zi[:, sl],
                                        bsum[:, sl], state_ref[h], consts[1][0])
            state_ref[h] = new_state
            ys.append((_rmsnorm(o, g_hgrn) * og[:, sl]).astype(BF16))
        return jnp.concatenate(ys, axis=1)

    def pool_branch(b, u):
        ext = jnp.concatenate([hist_ref[b], u], axis=0)
        hist_ref[b] = u[T - POOL_HISTORY:, :]
        acc = ext
        span = 1
        ys = []
        for gi, w in enumerate(POOL_WINDOWS):
            while span < w:
                acc = acc + pltpu.roll(acc, span, 0)
                span *= 2
            sl = slice(gi * POOL_GROUP_DIM, (gi + 1) * POOL_GROUP_DIM)
            pooled = acc[POOL_HISTORY:, sl] * (1.0 / w) - u[:, sl]
            y = _dot(pooled.astype(BF16), pool_w_ref[gi]) * pool_scale_ref[:, sl]
            ys.append(y.astype(BF16))
        return jnp.concatenate(ys, axis=1)

    n_blocks = D_MODEL // DENSE_COLS

    def enqueue_gated_b(b, z_pool, z_gb, z_ga):
        hold = {}
        queue.append((("side", b), lambda: hold.update(yb=pool_branch(b, assemble(z_pool)))))
        branch_b = enqueue_dot(("side", b), lambda: hold["yb"], w_b_ref, 0, D_MODEL)
        gate_a, gated_b = [], []

        def gates():
            gate_a.extend(_sigmoid(z_ga[cb]) for cb in range(n_blocks))
            gated_b.extend(_sigmoid(z_gb[cb]) * branch_b[cb] for cb in range(n_blocks))
        queue.append((("side", b), gates))
        return gate_a, gated_b

    def enqueue_outputs(b, ya, gate_a, gated_b):
        hold = {}
        queue.append((("out", b), lambda: hold.update(ya=jnp.concatenate(ya, axis=1))))
        branch_a = enqueue_dot(("out", b), lambda: hold["ya"], w_a_ref, 0, D_MODEL)

        def merge():
            merged = [gate_a[cb] * branch_a[cb] + gated_b[cb] for cb in range(n_blocks)]
            hold["merged"] = jnp.concatenate(merged, axis=1).astype(BF16)
        queue.append((("out", b), merge))
        for cb in range(n_blocks):
            def piece(cb=cb):
                cols = slice(cb * DENSE_COLS, (cb + 1) * DENSE_COLS)
                out_ref[b, :, cols] = xs[b][:, cols] + _dot(hold["merged"], w_out_ref[:, cols])
            queue.append((("out", b), piece))

    n_pairs = HEADS // 2
    z = {(0, 0): enqueue_pair(0, 0)}
    flush()
    others, sides = {}, {}
    for b in range(B):
        for p in range(1, n_pairs):
            z[(b, p)] = enqueue_pair(b, p)
        if b + 1 < B:
            z[(b + 1, 0)] = enqueue_pair(b + 1, 0)
        others[b] = tuple(enqueue_projection(("other", b), b, off, width) for off, width in
                          ((OFF_POOL, POOL_WIDTH), (OFF_GB, D_MODEL), (OFF_GA, D_MODEL)))
        if b + 1 < B:
            sides[b] = enqueue_gated_b(b, *others[b])
    for b in range(B):
        ya = []
        for p in range(n_pairs):
            flush(("pair", b, p))
            ya.append(hgrn_pair(b, p, *(assemble(blocks) for blocks in z[(b, p)])))
        if b not in sides:
            sides[b] = enqueue_gated_b(b, *others[b])
        enqueue_outputs(b, ya, *sides[b])
    flush()


def _ffn_kernel(h_ref, g_ffn_ref, w_gate_ref, w_up_ref, w_down_ref, g_final_ref, out_ref):
    groups = [slice(r * FFN_ROWS, (r + 1) * FFN_ROWS) for r in range(FFN_BLOCK // FFN_ROWS)]

    def normed(rows):
        return _rmsnorm(h_ref[rows, :], g_ffn_ref[...]).astype(BF16)

    def finish(rows, down):
        out_ref[rows, :] = _rmsnorm(h_ref[rows, :] + down, g_final_ref[...])

    def activation(n):
        d_ff = w_gate_ref.shape[1]
        acts = []
        for c0 in range(0, d_ff, DENSE_COLS):
            cols = slice(c0, c0 + DENSE_COLS)
            gate = _dot(n, w_gate_ref[:, cols])
            up = _dot(n, w_up_ref[:, cols])
            acts.append((_silu(gate) * up).astype(BF16))
        return jnp.concatenate(acts, axis=1)

    n = normed(groups[0])
    pending = None
    for i, rows in enumerate(groups):
        act = activation(n)
        if pending is not None:
            finish(*pending)
        if i + 1 < len(groups):
            n = normed(groups[i + 1])
        pending = (rows, _dot(act, w_down_ref[...]))
    finish(*pending)


def _resident(shape):
    zeros = (0,) * len(shape)
    return pl.BlockSpec(shape, lambda *_: zeros, pipeline_mode=pl.Buffered(1))


def _vmem_limit(resident, streamed):
    def nbytes(arrays):
        total = 0
        for shape, dtype in arrays:
            size = jnp.dtype(dtype).itemsize
            for dim in shape:
                size *= dim
            total += size
        return total

    limit = nbytes(resident) + 2 * nbytes(streamed) + VMEM_TEMPORARIES_BYTES
    assert limit <= VMEM_BYTES, "blocks and resident weights do not fit VMEM"
    return limit


def _row_slabs(rows, cols, steps):
    slab = next(r for r in range(BF16_ROWS, rows + 1, BF16_ROWS)
                if rows % r == 0 and rows // r <= steps)
    last = rows // slab - 1
    return pl.BlockSpec((slab, cols), lambda t: (jnp.minimum(t, last), 0))


def kernel(x, meta_tokens, norm_mix_g, w_in, lb_raw, hgrn_norm_g, pool_w, pool_scale,
           w_branch_a, w_branch_b, w_out, norm_ffn_g, w_ffn_gate, w_ffn_up, w_ffn_down,
           norm_final_g):
    B, S, D = x.shape
    assert D == D_MODEL and S % MIXER_BLOCK == 0 and (B * S) % FFN_BLOCK == 0
    assert w_in.shape == (1, D_MODEL, IN_WIDTH) and meta_tokens.shape == (N_META, D_MODEL)
    d_ff = w_ffn_gate.shape[-1]

    w_in_bf = w_in[0].astype(BF16)
    g_mix = norm_mix_g[0].reshape(1, D)
    lb_raw = lb_raw.astype(F32)

    T = MIXER_BLOCK
    steps = S // T
    ffn_weight_specs = [_row_slabs(D, d_ff, steps), _row_slabs(D, d_ff, steps),
                        _row_slabs(d_ff, D, steps)]
    h1, w_gate_bf, w_up_bf, w_down_bf = pl.pallas_call(
        _mixer_kernel,
        grid=(steps,),
        in_specs=[
            pl.BlockSpec((B, T, D), lambda t: (0, t, 0)),
            _resident((1, D)),
            _resident((D, IN_WIDTH)),
            _resident((2, HGRN_WIDTH)),
            _resident((1, HEAD_DIM)),
            _resident((len(POOL_WINDOWS), POOL_GROUP_DIM, POOL_GROUP_DIM)),
            _resident((1, POOL_WIDTH)),
            _resident((HGRN_WIDTH, D)),
            _resident((POOL_WIDTH, D)),
            _resident((D, D)),
            _resident((N_META, D)),
        ] + ffn_weight_specs,
        out_specs=[pl.BlockSpec((B, T, D), lambda t: (0, t, 0))] + ffn_weight_specs,
        out_shape=(jax.ShapeDtypeStruct((B, S, D), F32),
                   jax.ShapeDtypeStruct((D, d_ff), BF16),
                   jax.ShapeDtypeStruct((D, d_ff), BF16),
                   jax.ShapeDtypeStruct((d_ff, D), BF16)),
        scratch_shapes=[
            pltpu.VMEM((B * HEADS, HEAD_DIM, HEAD_DIM), F32),
            pltpu.VMEM((B, POOL_HISTORY, POOL_WIDTH), F32),
        ],
        compiler_params=pltpu.CompilerParams(
            dimension_semantics=("arbitrary",),
            vmem_limit_bytes=_vmem_limit(
                [((D, IN_WIDTH + HGRN_WIDTH + POOL_WIDTH + D), BF16),
                 ((B * HEADS, HEAD_DIM, HEAD_DIM), F32)],
                [((B, T, D), F32)] * 2
                + [(spec.block_shape, dtype) for spec in ffn_weight_specs
                   for dtype in (F32, BF16)])),
        name="mixer",
    )(x, g_mix, w_in_bf, lb_raw, hgrn_norm_g[0].reshape(1, HEAD_DIM),
      pool_w[0].astype(BF16), pool_scale[0].reshape(1, POOL_WIDTH),
      w_branch_a[0].astype(BF16), w_branch_b[0].astype(BF16), w_out[0].astype(BF16),
      meta_tokens, w_ffn_gate[0], w_ffn_up[0], w_ffn_down[0])

    TM = FFN_BLOCK
    out = pl.pallas_call(
        _ffn_kernel,
        grid=(B * S // TM,),
        in_specs=[
            pl.BlockSpec((TM, D), lambda i: (i, 0)),
            _resident((1, D)),
            _resident((D, d_ff)),
            _resident((D, d_ff)),
            _resident((d_ff, D)),
            _resident((1, D)),
        ],
        out_specs=pl.BlockSpec((TM, D), lambda i: (i, 0)),
        out_shape=jax.ShapeDtypeStruct((B * S, D), F32),
        compiler_params=pltpu.CompilerParams(
            dimension_semantics=("arbitrary",),
            vmem_limit_bytes=_vmem_limit([((D, 3 * d_ff), BF16)], [((TM, D), F32)] * 2)),
        name="ffn",
    )(h1.reshape(B * S, D), norm_ffn_g[0].reshape(1, D), w_gate_bf, w_up_bf, w_down_bf,
      norm_final_g.reshape(1, D))
    return out.reshape(B, S, D)
```

```python
import jax
import jax.numpy as jnp
from jax import lax
from jax.experimental import pallas as pl
from jax.experimental.pallas import tpu as pltpu

D_MODEL = 1024
N_META = 16
HEADS = 8
HEAD_DIM = 128
HGRN_WIDTH = HEADS * HEAD_DIM
POOL_WINDOWS = (2, 4, 8, 16)
POOL_GROUP_DIM = 128
POOL_WIDTH = len(POOL_WINDOWS) * POOL_GROUP_DIM
POOL_HISTORY = 16
EPS = 1e-6

OFF_Q = 0
OFF_F = OFF_Q + HGRN_WIDTH
OFF_I = OFF_F + HGRN_WIDTH
OFF_OG = OFF_I + HGRN_WIDTH
OFF_POOL = OFF_OG + HGRN_WIDTH
OFF_GA = OFF_POOL + POOL_WIDTH
OFF_GB = OFF_GA + D_MODEL
IN_WIDTH = OFF_GB + D_MODEL

SUBLANES = 8
BF16_ROWS = 2 * SUBLANES
MIXER_BLOCK = 256
SCORE_CHUNK = 128
DENSE_COLS = 256
MID_LEVEL = 8
FFN_BLOCK = 1024
FFN_ROWS = 256
VMEM_BYTES = 64 * 1024 * 1024
VMEM_TEMPORARIES_BYTES = 16 * 1024 * 1024

F32 = jnp.float32
BF16 = jnp.bfloat16


def _rmsnorm(x, g):
    ms = jnp.mean(x * x, axis=-1, keepdims=True)
    return x * lax.rsqrt(ms + EPS) * g


def _sigmoid(x):
    return 0.5 * jnp.tanh(0.5 * x) + 0.5


def _silu(x):
    h = 0.5 * x
    return h * jnp.tanh(h) + h


def _dot(a, b):
    return jnp.dot(a, b, preferred_element_type=F32)


def _dot_nt(a, b):
    return lax.dot_general(a, b, (((1,), (1,)), ((), ())), preferred_element_type=F32)


def _dot_tn(a, b):
    return lax.dot_general(a, b, (((0,), (0,)), ((), ())), preferred_element_type=F32)


def _lower_bound(lb_raw):
    r0, r1 = lb_raw[0:1, :], lb_raw[1:2, :]
    m = jnp.maximum(r0, r1)
    e0, e1 = jnp.exp(r0 - m), jnp.exp(r1 - m)
    return e0 / (e0 + e1)


def _cumsum_rows(g, tri):
    g1 = g.astype(BF16)
    g2 = (g - g1.astype(F32)).astype(BF16)
    return _dot(tri, g1) + _dot(tri, g2)


def _tri(n):
    r = lax.broadcasted_iota(jnp.int32, (n, n), 0)
    c = lax.broadcasted_iota(jnp.int32, (n, n), 1)
    return (c <= r).astype(BF16)


def _forget_gate(zf, lb):
    f = lb + (1.0 - lb) * _sigmoid(zf)
    return 1.0 - f, f, jnp.log2(f)


def _meta_state_kernel(meta_ref, g_ref, wf_ref, wi_ref, wp_ref, lb_raw_ref,
                       state_ref, hist_ref):
    n = _rmsnorm(meta_ref[...], g_ref[...]).astype(BF16)
    lb = _lower_bound(lb_raw_ref[...])
    k, _, g = _forget_gate(_dot(n, wf_ref[...]), lb)
    v = _dot(n, wi_ref[...])
    hist_ref[...] = _dot(n, wp_ref[...])
    b = _cumsum_rows(g, _tri(N_META))
    k_end = (k * jnp.exp2(b[N_META - 1:N_META, :] - b)).astype(BF16)
    v = v.astype(BF16)
    for h in range(HEADS):
        sl = slice(h * HEAD_DIM, (h + 1) * HEAD_DIM)
        state_ref[h] = _dot_tn(v[:, sl], k_end[:, sl])


def _sibling_boundary(b, m, sub):
    T = b.shape[0]
    if m >= SUBLANES:
        nb = T // (2 * m)
        r = b.reshape(nb, 2 * m, HEAD_DIM)[:, m - 1:m, :]
        return jnp.broadcast_to(r, (nb, 2 * m, HEAD_DIM)).reshape(T, HEAD_DIM)
    groups = b.reshape(T // SUBLANES, SUBLANES, HEAD_DIM)

    def bcast(i):
        r = jnp.broadcast_to(groups[:, i:i + 1, :], groups.shape)
        return r.reshape(T, HEAD_DIM)

    r = bcast(m - 1)
    for first in range(2 * m, SUBLANES, 2 * m):
        r = jnp.where(sub >= first, bcast(first + m - 1), r)
    return r


def _level_constants(T):
    row = lax.broadcasted_iota(jnp.int32, (T, HEAD_DIM), 0)
    r_i = lax.broadcasted_iota(jnp.int32, (SCORE_CHUNK, SCORE_CHUNK), 0)
    c_i = lax.broadcasted_iota(jnp.int32, (SCORE_CHUNK, SCORE_CHUNK), 1)
    consts = {}
    m = 1
    while m < T:
        later = ((row // m) % 2) == 1
        sign = jnp.where(later, 1.0, -1.0)
        mask = None
        if m < SCORE_CHUNK:
            mask = (((r_i // (2 * m)) == (c_i // (2 * m)))
                    & ((r_i // m) % 2 == 1) & ((c_i // m) % 2 == 0))
        consts[m] = (later, sign, mask)
        m *= 2
    return row % SUBLANES, consts


def _hgrn_scores(q, k, f, b, state_t, sub, consts, tick):
    T = q.shape[0]
    n_chunks = T // SCORE_CHUNK
    o_prev = _dot((q * jnp.exp2(b)).astype(BF16), state_t.T.astype(BF16))

    blocks = [[None] * n_chunks for _ in range(n_chunks)]
    m = 2
    while m < T:
        later, sign, mask = consts[m]
        r = _sibling_boundary(b, m, sub)
        z32 = jnp.where(later, q, k) * jnp.exp2((b - r) * sign)
        z = z32.astype(BF16)
        zc = [z[c * SCORE_CHUNK:(c + 1) * SCORE_CHUNK] for c in range(n_chunks)]
        zt = [z32[c * SCORE_CHUNK:(c + 1) * SCORE_CHUNK].T.astype(BF16)
              for c in range(n_chunks - (m >= SCORE_CHUNK))]
        if m < BF16_ROWS:
            for c in range(n_chunks):
                s = _dot(zc[c], zt[c])
                blocks[c][c] = jnp.where(mask, s, 0.0 if blocks[c][c] is None else blocks[c][c])
        elif m < SCORE_CHUNK:
            pairs = SCORE_CHUNK // (2 * m)
            for c in range(n_chunks):
                queries = jnp.concatenate(
                    [zc[c][(2 * j + 1) * m:(2 * j + 2) * m] for j in range(pairs)], axis=0)
                s = _dot(queries, zt[c])
                prev = blocks[c][c]
                rows = []
                for j in range(pairs):
                    lo, hi = (2 * j + 1) * m, (2 * j + 2) * m
                    rows.append(prev[lo - m:lo])
                    rows.append(jnp.where(mask[lo:hi], s[j * m:(j + 1) * m], prev[lo:hi]))
                blocks[c][c] = jnp.concatenate(rows, axis=0)
        else:
            span = m // SCORE_CHUNK
            for i in range(n_chunks):
                if (i // span) % 2 == 1:
                    for j in range((i // span - 1) * span, (i // span) * span):
                        blocks[i][j] = _dot(zc[i], zt[j])
        if m == MID_LEVEL:
            tick()
        m *= 2
    return o_prev, blocks


def _previous_row(x):
    T = x.shape[0]
    groups = x.reshape(T // SUBLANES, SUBLANES, HEAD_DIM)
    return pltpu.roll(groups, 1, 1).reshape(T, HEAD_DIM)


def _hgrn_output(o_prev, blocks, q, k, f, v, b, state_t, odd):
    T = q.shape[0]
    n_chunks = T // SCORE_CHUNK
    v_bf = v.astype(BF16)
    strips = []
    for i in range(n_chunks):
        a = jnp.concatenate(blocks[i][:i + 1], axis=1) if i else blocks[0][0]
        strips.append(_dot(a.astype(BF16), v_bf[:(i + 1) * SCORE_CHUNK]))
    o = o_prev + (jnp.concatenate(strips, axis=0) if n_chunks > 1 else strips[0])
    o = o + jnp.sum(q * k, axis=-1, keepdims=True) * v
    pair = jnp.where(odd, q * f, 0.0) * _previous_row(k)
    o = o + jnp.sum(pair, axis=-1, keepdims=True) * _previous_row(v)

    b_end = b[T - 1:T, :]
    k_end = (k * jnp.exp2(b_end - b)).astype(BF16)
    new_state_t = state_t * jnp.exp2(b_end) + _dot_tn(v_bf, k_end)
    return o, new_state_t


def _mixer_kernel(x_ref, g_mix_ref, w_in_ref, lb_raw_ref, g_hgrn_ref, pool_w_ref, pool_scale_ref,
                  w_a_ref, w_b_ref, w_out_ref, state0_ref, hist0_ref,
                  ffn_gate_ref, ffn_up_ref, ffn_down_ref,
                  out_ref, ffn_gate_bf_ref, ffn_up_bf_ref, ffn_down_bf_ref,
                  state_ref, hist_ref):
    T = MIXER_BLOCK
    B = x_ref.shape[0]

    @pl.when(pl.program_id(0) == 0)
    def _():
        for b in range(B):
            for h in range(HEADS):
                state_ref[b * HEADS + h] = state0_ref[h]
            hist_ref[b] = hist0_ref[...]

    ffn_gate_bf_ref[...] = ffn_gate_ref[...].astype(BF16)
    ffn_up_bf_ref[...] = ffn_up_ref[...].astype(BF16)
    ffn_down_bf_ref[...] = ffn_down_ref[...].astype(BF16)

    xs = [x_ref[b] for b in range(B)]
    ns = [_rmsnorm(x, g_mix_ref[...]).astype(BF16) for x in xs]
    lb = _lower_bound(lb_raw_ref[...])
    tri = _tri(T)
    sub, consts = _level_constants(T)
    g_hgrn = g_hgrn_ref[...]
    PAIR = 2 * HEAD_DIM

    queue = []

    def tick(n=1):
        for _ in range(n):
            if queue:
                queue.pop(0)[1]()

    def flush(tag=None):
        while queue if tag is None else any(t == tag for t, _ in queue):
            queue.pop(0)[1]()

    def enqueue_dot(tag, lhs, rhs_ref, col0, n_cols):
        blocks = [None] * (n_cols // DENSE_COLS)
        for cb in range(len(blocks)):
            def piece(cb=cb):
                cols = slice(col0 + cb * DENSE_COLS, col0 + (cb + 1) * DENSE_COLS)
                blocks[cb] = _dot(lhs(), rhs_ref[:, cols])
            queue.append((tag, piece))
        return blocks

    def assemble(blocks):
        return jnp.concatenate(blocks, axis=1) if len(blocks) > 1 else blocks[0]

    def enqueue_projection(tag, b, offset, width):
        return enqueue_dot(tag, lambda: ns[b], w_in_ref, offset, width)

    def enqueue_pair(b, p):
        return tuple(enqueue_projection(("pair", b, p), b, off + p * PAIR, PAIR)
                     for off in (OFF_Q, OFF_F, OFF_I, OFF_OG))

    def hgrn_pair(b, p, zq, zf, zi, zog):
        k, f, g = _forget_gate(zf, lb[:, p * PAIR:(p + 1) * PAIR])
        bsum = _cumsum_rows(g, tri)
        tick(2)
        q = _silu(zq)
        og = _silu(zog)
        heads = [(b * HEADS + 2 * p + j, slice(j * HEAD_DIM, (j + 1) * HEAD_DIM))
                 for j in range(2)]
        scores = []
        for h, sl in heads:
            scores.append(_hgrn_scores(q[:, sl], k[:, sl], f[:, sl], bsum[:, sl], state_ref[h],
                                       sub, consts, tick))
            tick()
        tick(2)
        ys = []
        for (h, sl), (o_prev, blocks) in zip(heads, scores):
            o, new_state = _hgrn_output(o_prev, blocks, q[:, sl], k[:, sl], f[:, sl], zi[:, sl],
                                        bsum[:, sl], state_ref[h], consts[1][0])
            state_ref[h] = new_state
            ys.append((_rmsnorm(o, g_hgrn) * og[:, sl]).astype(BF16))
        return jnp.concatenate(ys, axis=1)

    def pool_branch(b, u):
        ext = jnp.concatenate([hist_ref[b], u], axis=0)
        hist_ref[b] = u[T - POOL_HISTORY:, :]
        acc = ext
        span = 1
        ys = []
        for gi, w in enumerate(POOL_WINDOWS):
            while span < w:
                acc = acc + pltpu.roll(acc, span, 0)
                span *= 2
            sl = slice(gi * POOL_GROUP_DIM, (gi + 1) * POOL_GROUP_DIM)
            pooled = acc[POOL_HISTORY:, sl] * (1.0 / w) - u[:, sl]
            y = _dot(pooled.astype(BF16), pool_w_ref[gi]) * pool_scale_ref[:, sl]
            ys.append(y.astype(BF16))
        return jnp.concatenate(ys, axis=1)

    n_blocks = D_MODEL // DENSE_COLS

    def enqueue_gated_b(b, z_pool, z_gb, z_ga):
        hold = {}
        queue.append((("side", b), lambda: hold.update(yb=pool_branch(b, assemble(z_pool)))))
        branch_b = enqueue_dot(("side", b), lambda: hold["yb"], w_b_ref, 0, D_MODEL)
        gate_a, gated_b = [], []

        def gates():
            gate_a.extend(_sigmoid(z_ga[cb]) for cb in range(n_blocks))
            gated_b.extend(_sigmoid(z_gb[cb]) * branch_b[cb] for cb in range(n_blocks))
        queue.append((("side", b), gates))
        return gate_a, gated_b

    def enqueue_outputs(b, ya, gate_a, gated_b):
        hold = {}
        queue.append((("out", b), lambda: hold.update(ya=jnp.concatenate(ya, axis=1))))
        branch_a = enqueue_dot(("out", b), lambda: hold["ya"], w_a_ref, 0, D_MODEL)

        def merge():
            merged = [gate_a[cb] * branch_a[cb] + gated_b[cb] for cb in range(n_blocks)]
            hold["merged"] = jnp.concatenate(merged, axis=1).astype(BF16)
        queue.append((("out", b), merge))
        for cb in range(n_blocks):
            def piece(cb=cb):
                cols = slice(cb * DENSE_COLS, (cb + 1) * DENSE_COLS)
                out_ref[b, :, cols] = xs[b][:, cols] + _dot(hold["merged"], w_out_ref[:, cols])
            queue.append((("out", b), piece))

    n_pairs = HEADS // 2
    z = {(0, 0): enqueue_pair(0, 0)}
    flush()
    others, sides = {}, {}
    for b in range(B):
        for p in range(1, n_pairs):
            z[(b, p)] = enqueue_pair(b, p)
        if b + 1 < B:
            z[(b + 1, 0)] = enqueue_pair(b + 1, 0)
        others[b] = tuple(enqueue_projection(("other", b), b, off, width) for off, width in
                          ((OFF_POOL, POOL_WIDTH), (OFF_GB, D_MODEL), (OFF_GA, D_MODEL)))
        if b + 1 < B:
            sides[b] = enqueue_gated_b(b, *others[b])
    for b in range(B):
        ya = []
        for p in range(n_pairs):
            flush(("pair", b, p))
            ya.append(hgrn_pair(b, p, *(assemble(blocks) for blocks in z[(b, p)])))
        if b not in sides:
            sides[b] = enqueue_gated_b(b, *others[b])
        enqueue_outputs(b, ya, *sides[b])
    flush()


def _ffn_kernel(h_ref, g_ffn_ref, w_gate_ref, w_up_ref, w_down_ref, g_final_ref, out_ref):
    groups = [slice(r * FFN_ROWS, (r + 1) * FFN_ROWS) for r in range(FFN_BLOCK // FFN_ROWS)]

    def normed(rows):
        return _rmsnorm(h_ref[rows, :], g_ffn_ref[...]).astype(BF16)

    def finish(rows, down):
        out_ref[rows, :] = _rmsnorm(h_ref[rows, :] + down, g_final_ref[...])

    def activation(n):
        d_ff = w_gate_ref.shape[1]
        acts = []
        for c0 in range(0, d_ff, DENSE_COLS):
            cols = slice(c0, c0 + DENSE_COLS)
            gate = _dot(n, w_gate_ref[:, cols])
            up = _dot(n, w_up_ref[:, cols])
            acts.append((_silu(gate) * up).astype(BF16))
        return jnp.concatenate(acts, axis=1)

    n = normed(groups[0])
    pending = None
    for i, rows in enumerate(groups):
        act = activation(n)
        if pending is not None:
            finish(*pending)
        if i + 1 < len(groups):
            n = normed(groups[i + 1])
        pending = (rows, _dot(act, w_down_ref[...]))
    finish(*pending)


def _resident(shape):
    zeros = (0,) * len(shape)
    return pl.BlockSpec(shape, lambda *_: zeros, pipeline_mode=pl.Buffered(1))


def _vmem_limit(resident, streamed):
    def nbytes(arrays):
        total = 0
        for shape, dtype in arrays:
            size = jnp.dtype(dtype).itemsize
            for dim in shape:
                size *= dim
            total += size
        return total

    limit = nbytes(resident) + 2 * nbytes(streamed) + VMEM_TEMPORARIES_BYTES
    assert limit <= VMEM_BYTES, "blocks and resident weights do not fit VMEM"
    return limit


def _row_slabs(rows, cols, steps):
    slab = next(r for r in range(BF16_ROWS, rows + 1, BF16_ROWS)
                if rows % r == 0 and rows // r <= steps)
    last = rows // slab - 1
    return pl.BlockSpec((slab, cols), lambda t: (jnp.minimum(t, last), 0))


def kernel(x, meta_tokens, norm_mix_g, w_in, lb_raw, hgrn_norm_g, pool_w, pool_scale,
           w_branch_a, w_branch_b, w_out, norm_ffn_g, w_ffn_gate, w_ffn_up, w_ffn_down,
           norm_final_g):
    B, S, D = x.shape
    assert D == D_MODEL and S % MIXER_BLOCK == 0 and (B * S) % FFN_BLOCK == 0
    assert w_in.shape == (1, D_MODEL, IN_WIDTH) and meta_tokens.shape == (N_META, D_MODEL)
    d_ff = w_ffn_gate.shape[-1]

    w_in_bf = w_in[0].astype(BF16)
    g_mix = norm_mix_g[0].reshape(1, D)
    lb_raw = lb_raw.astype(F32)

    def whole(shape):
        zeros = (0,) * len(shape)
        return pl.BlockSpec(shape, lambda i: zeros)

    def w_in_columns(offset, width):
        assert offset % width == 0
        return pl.BlockSpec((D, width), lambda i: (0, offset // width))

    state0, hist0 = pl.pallas_call(
        _meta_state_kernel,
        grid=(1,),
        in_specs=[whole((N_META, D)), whole((1, D)),
                  w_in_columns(OFF_F, HGRN_WIDTH), w_in_columns(OFF_I, HGRN_WIDTH),
                  w_in_columns(OFF_POOL, POOL_WIDTH), whole((2, HGRN_WIDTH))],
        out_specs=(whole((HEADS, HEAD_DIM, HEAD_DIM)), whole((N_META, POOL_WIDTH))),
        out_shape=(jax.ShapeDtypeStruct((HEADS, HEAD_DIM, HEAD_DIM), F32),
                   jax.ShapeDtypeStruct((N_META, POOL_WIDTH), F32)),
        compiler_params=pltpu.CompilerParams(vmem_limit_bytes=_vmem_limit(
            [], [((D, 2 * HGRN_WIDTH + POOL_WIDTH), BF16), ((N_META, D), F32)])),
        name="meta_state",
    )(meta_tokens, g_mix, w_in_bf, w_in_bf, w_in_bf, lb_raw)

    T = MIXER_BLOCK
    steps = S // T
    ffn_weight_specs = [_row_slabs(D, d_ff, steps), _row_slabs(D, d_ff, steps),
                        _row_slabs(d_ff, D, steps)]
    h1, w_gate_bf, w_up_bf, w_down_bf = pl.pallas_call(
        _mixer_kernel,
        grid=(steps,),
        in_specs=[
            pl.BlockSpec((B, T, D), lambda t: (0, t, 0)),
            _resident((1, D)),
            _resident((D, IN_WIDTH)),
            _resident((2, HGRN_WIDTH)),
            _resident((1, HEAD_DIM)),
            _resident((len(POOL_WINDOWS), POOL_GROUP_DIM, POOL_GROUP_DIM)),
            _resident((1, POOL_WIDTH)),
            _resident((HGRN_WIDTH, D)),
            _resident((POOL_WIDTH, D)),
            _resident((D, D)),
            _resident((HEADS, HEAD_DIM, HEAD_DIM)),
            _resident((N_META, POOL_WIDTH)),
        ] + ffn_weight_specs,
        out_specs=[pl.BlockSpec((B, T, D), lambda t: (0, t, 0))] + ffn_weight_specs,
        out_shape=(jax.ShapeDtypeStruct((B, S, D), F32),
                   jax.ShapeDtypeStruct((D, d_ff), BF16),
                   jax.ShapeDtypeStruct((D, d_ff), BF16),
                   jax.ShapeDtypeStruct((d_ff, D), BF16)),
        scratch_shapes=[
            pltpu.VMEM((B * HEADS, HEAD_DIM, HEAD_DIM), F32),
            pltpu.VMEM((B, POOL_HISTORY, POOL_WIDTH), F32),
        ],
        compiler_params=pltpu.CompilerParams(
            dimension_semantics=("arbitrary",),
            vmem_limit_bytes=_vmem_limit(
                [((D, IN_WIDTH + HGRN_WIDTH + POOL_WIDTH + D), BF16),
                 (((B + 1) * HEADS, HEAD_DIM, HEAD_DIM), F32)],
                [((B, T, D), F32)] * 2
                + [(spec.block_shape, dtype) for spec in ffn_weight_specs
                   for dtype in (F32, BF16)])),
        name="mixer",
    )(x, g_mix, w_in_bf, lb_raw, hgrn_norm_g[0].reshape(1, HEAD_DIM),
      pool_w[0].astype(BF16), pool_scale[0].reshape(1, POOL_WIDTH),
      w_branch_a[0].astype(BF16), w_branch_b[0].astype(BF16), w_out[0].astype(BF16),
      state0, hist0, w_ffn_gate[0], w_ffn_up[0], w_ffn_down[0])

    TM = FFN_BLOCK
    out = pl.pallas_call(
        _ffn_kernel,
        grid=(B * S // TM,),
        in_specs=[
            pl.BlockSpec((TM, D), lambda i: (i, 0)),
            _resident((1, D)),
            _resident((D, d_ff)),
            _resident((D, d_ff)),
            _resident((d_ff, D)),
            _resident((1, D)),
        ],
        out_specs=pl.BlockSpec((TM, D), lambda i: (i, 0)),
        out_shape=jax.ShapeDtypeStruct((B * S, D), F32),
        compiler_params=pltpu.CompilerParams(
            dimension_semantics=("arbitrary",),
            vmem_limit_bytes=_vmem_limit([((D, 3 * d_ff), BF16)], [((TM, D), F32)] * 2)),
        name="ffn",
    )(h1.reshape(B * S, D), norm_ffn_g[0].reshape(1, D), w_gate_bf, w_up_bf, w_down_bf,
      norm_final_g.reshape(1, D))
    return out.reshape(B, S, D)
```

```python
import jax
import jax.numpy as jnp
from jax import lax
from jax.experimental import pallas as pl
from jax.experimental.pallas import tpu as pltpu

D_MODEL = 1024
N_META = 16
HEADS = 8
HEAD_DIM = 128
HGRN_WIDTH = HEADS * HEAD_DIM
POOL_WINDOWS = (2, 4, 8, 16)
POOL_GROUP_DIM = 128
POOL_WIDTH = len(POOL_WINDOWS) * POOL_GROUP_DIM
POOL_HISTORY = 16
EPS = 1e-6

OFF_Q = 0
OFF_F = OFF_Q + HGRN_WIDTH
OFF_I = OFF_F + HGRN_WIDTH
OFF_OG = OFF_I + HGRN_WIDTH
OFF_POOL = OFF_OG + HGRN_WIDTH
OFF_GA = OFF_POOL + POOL_WIDTH
OFF_GB = OFF_GA + D_MODEL
IN_WIDTH = OFF_GB + D_MODEL

SUBLANES = 8
BF16_ROWS = 2 * SUBLANES
MIXER_BLOCK = 256
SCORE_CHUNK = 128
DENSE_COLS = 256
MID_LEVEL = 8
FFN_BLOCK = 1024
FFN_ROWS = 256
VMEM_BYTES = 64 * 1024 * 1024
VMEM_TEMPORARIES_BYTES = 16 * 1024 * 1024

F32 = jnp.float32
BF16 = jnp.bfloat16


def _rmsnorm(x, g):
    ms = jnp.mean(x * x, axis=-1, keepdims=True)
    return x * lax.rsqrt(ms + EPS) * g


def _sigmoid(x):
    return 0.5 * jnp.tanh(0.5 * x) + 0.5


def _silu(x):
    h = 0.5 * x
    return h * jnp.tanh(h) + h


def _dot(a, b):
    return jnp.dot(a, b, preferred_element_type=F32)


def _dot_nt(a, b):
    return lax.dot_general(a, b, (((1,), (1,)), ((), ())), preferred_element_type=F32)


def _dot_tn(a, b):
    return lax.dot_general(a, b, (((0,), (0,)), ((), ())), preferred_element_type=F32)


def _lower_bound(lb_raw):
    r0, r1 = lb_raw[0:1, :], lb_raw[1:2, :]
    m = jnp.maximum(r0, r1)
    e0, e1 = jnp.exp(r0 - m), jnp.exp(r1 - m)
    return e0 / (e0 + e1)


def _cumsum_rows(g, tri):
    g1 = g.astype(BF16)
    g2 = (g - g1.astype(F32)).astype(BF16)
    return _dot(tri, g1) + _dot(tri, g2)


def _tri(n):
    r = lax.broadcasted_iota(jnp.int32, (n, n), 0)
    c = lax.broadcasted_iota(jnp.int32, (n, n), 1)
    return (c <= r).astype(BF16)


def _forget_gate(zf, lb):
    f = lb + (1.0 - lb) * _sigmoid(zf)
    return 1.0 - f, f, jnp.log2(f)


def _meta_state_kernel(meta_ref, g_ref, wf_ref, wi_ref, wp_ref, lb_raw_ref,
                       state_ref, hist_ref):
    n = _rmsnorm(meta_ref[...], g_ref[...]).astype(BF16)
    lb = _lower_bound(lb_raw_ref[...])
    k, _, g = _forget_gate(_dot(n, wf_ref[...]), lb)
    v = _dot(n, wi_ref[...])
    hist_ref[...] = _dot(n, wp_ref[...])
    b = _cumsum_rows(g, _tri(N_META))
    k_end = (k * jnp.exp2(b[N_META - 1:N_META, :] - b)).astype(BF16)
    v = v.astype(BF16)
    for h in range(HEADS):
        sl = slice(h * HEAD_DIM, (h + 1) * HEAD_DIM)
        state_ref[h] = _dot_tn(v[:, sl], k_end[:, sl])


def _sibling_boundary(b, m, sub):
    T = b.shape[0]
    if m >= SUBLANES:
        nb = T // (2 * m)
        r = b.reshape(nb, 2 * m, HEAD_DIM)[:, m - 1:m, :]
        return jnp.broadcast_to(r, (nb, 2 * m, HEAD_DIM)).reshape(T, HEAD_DIM)
    groups = b.reshape(T // SUBLANES, SUBLANES, HEAD_DIM)

    def bcast(i):
        r = jnp.broadcast_to(groups[:, i:i + 1, :], groups.shape)
        return r.reshape(T, HEAD_DIM)

    r = bcast(m - 1)
    for first in range(2 * m, SUBLANES, 2 * m):
        r = jnp.where(sub >= first, bcast(first + m - 1), r)
    return r


def _level_constants(T):
    row = lax.broadcasted_iota(jnp.int32, (T, HEAD_DIM), 0)
    r_i = lax.broadcasted_iota(jnp.int32, (SCORE_CHUNK, SCORE_CHUNK), 0)
    c_i = lax.broadcasted_iota(jnp.int32, (SCORE_CHUNK, SCORE_CHUNK), 1)
    consts = {}
    m = 1
    while m < T:
        later = ((row // m) % 2) == 1
        sign = jnp.where(later, 1.0, -1.0)
        mask = None
        if m < SCORE_CHUNK:
            mask = (((r_i // (2 * m)) == (c_i // (2 * m)))
                    & ((r_i // m) % 2 == 1) & ((c_i // m) % 2 == 0))
        consts[m] = (later, sign, mask)
        m *= 2
    return row % SUBLANES, consts


def _hgrn_scores(q, k, f, b, state_t, sub, consts, tick):
    T = q.shape[0]
    n_chunks = T // SCORE_CHUNK
    o_prev = _dot((q * jnp.exp2(b)).astype(BF16), state_t.T.astype(BF16))

    blocks = [[None] * n_chunks for _ in range(n_chunks)]
    m = 2
    while m < T:
        later, sign, mask = consts[m]
        r = _sibling_boundary(b, m, sub)
        z32 = jnp.where(later, q, k) * jnp.exp2((b - r) * sign)
        z = z32.astype(BF16)
        zc = [z[c * SCORE_CHUNK:(c + 1) * SCORE_CHUNK] for c in range(n_chunks)]
        zt = [z32[c * SCORE_CHUNK:(c + 1) * SCORE_CHUNK].T.astype(BF16)
              for c in range(n_chunks - (m >= SCORE_CHUNK))]
        if m < BF16_ROWS:
            for c in range(n_chunks):
                s = _dot(zc[c], zt[c])
                blocks[c][c] = jnp.where(mask, s, 0.0 if blocks[c][c] is None else blocks[c][c])
        elif m < SCORE_CHUNK:
            pairs = SCORE_CHUNK // (2 * m)
            for c in range(n_chunks):
                queries = jnp.concatenate(
                    [zc[c][(2 * j + 1) * m:(2 * j + 2) * m] for j in range(pairs)], axis=0)
                s = _dot(queries, zt[c])
                prev = blocks[c][c]
                rows = []
                for j in range(pairs):
                    lo, hi = (2 * j + 1) * m, (2 * j + 2) * m
                    rows.append(prev[lo - m:lo])
                    rows.append(jnp.where(mask[lo:hi], s[j * m:(j + 1) * m], prev[lo:hi]))
                blocks[c][c] = jnp.concatenate(rows, axis=0)
        else:
            span = m // SCORE_CHUNK
            for i in range(n_chunks):
                if (i // span) % 2 == 1:
                    for j in range((i // span - 1) * span, (i // span) * span):
                        blocks[i][j] = _dot(zc[i], zt[j])
        if m == MID_LEVEL:
            tick()
        m *= 2
    return o_prev, blocks


def _previous_row(x):
    T = x.shape[0]
    groups = x.reshape(T // SUBLANES, SUBLANES, HEAD_DIM)
    return pltpu.roll(groups, 1, 1).reshape(T, HEAD_DIM)


def _hgrn_output(o_prev, blocks, q, k, f, v, b, state_t, odd):
    T = q.shape[0]
    n_chunks = T // SCORE_CHUNK
    v_bf = v.astype(BF16)
    strips = []
    for i in range(n_chunks):
        a = jnp.concatenate(blocks[i][:i + 1], axis=1) if i else blocks[0][0]
        strips.append(_dot(a.astype(BF16), v_bf[:(i + 1) * SCORE_CHUNK]))
    o = o_prev + (jnp.concatenate(strips, axis=0) if n_chunks > 1 else strips[0])
    o = o + jnp.sum(q * k, axis=-1, keepdims=True) * v
    pair = jnp.where(odd, q * f, 0.0) * _previous_row(k)
    o = o + jnp.sum(pair, axis=-1, keepdims=True) * _previous_row(v)

    b_end = b[T - 1:T, :]
    k_end = (k * jnp.exp2(b_end - b)).astype(BF16)
    new_state_t = state_t * jnp.exp2(b_end) + _dot_tn(v_bf, k_end)
    return o, new_state_t


def _mixer_kernel(x_ref, g_mix_ref, w_in_ref, lb_raw_ref, g_hgrn_ref, pool_w_ref, pool_scale_ref,
                  w_a_ref, w_b_ref, w_out_ref, state0_ref, hist0_ref,
                  ffn_gate_ref, ffn_up_ref, ffn_down_ref,
                  out_ref, ffn_gate_bf_ref, ffn_up_bf_ref, ffn_down_bf_ref,
                  state_ref, hist_ref):
    T = MIXER_BLOCK
    B = x_ref.shape[0]

    @pl.when(pl.program_id(0) == 0)
    def _():
        for b in range(B):
            for h in range(HEADS):
                state_ref[b * HEADS + h] = state0_ref[h]
            hist_ref[b] = hist0_ref[...]

    ffn_gate_bf_ref[...] = ffn_gate_ref[...].astype(BF16)
    ffn_up_bf_ref[...] = ffn_up_ref[...].astype(BF16)
    ffn_down_bf_ref[...] = ffn_down_ref[...].astype(BF16)

    xs = [x_ref[b] for b in range(B)]
    ns = [_rmsnorm(x, g_mix_ref[...]).astype(BF16) for x in xs]
    lb = _lower_bound(lb_raw_ref[...])
    tri = _tri(T)
    sub, consts = _level_constants(T)
    g_hgrn = g_hgrn_ref[...]
    PAIR = 2 * HEAD_DIM

    queue = []

    def tick(n=1):
        for _ in range(n):
            if queue:
                queue.pop(0)[1]()

    def flush(tag=None):
        while queue if tag is None else any(t == tag for t, _ in queue):
            queue.pop(0)[1]()

    def enqueue_dot(tag, lhs, rhs_ref, col0, n_cols):
        blocks = [None] * (n_cols // DENSE_COLS)
        for cb in range(len(blocks)):
            def piece(cb=cb):
                cols = slice(col0 + cb * DENSE_COLS, col0 + (cb + 1) * DENSE_COLS)
                blocks[cb] = _dot(lhs(), rhs_ref[:, cols])
            queue.append((tag, piece))
        return blocks

    def assemble(blocks):
        return jnp.concatenate(blocks, axis=1) if len(blocks) > 1 else blocks[0]

    def enqueue_projection(tag, b, offset, width):
        return enqueue_dot(tag, lambda: ns[b], w_in_ref, offset, width)

    def enqueue_pair(b, p):
        return tuple(enqueue_projection(("pair", b, p), b, off + p * PAIR, PAIR)
                     for off in (OFF_Q, OFF_F, OFF_I, OFF_OG))

    def hgrn_pair(b, p, zq, zf, zi, zog):
        k, f, g = _forget_gate(zf, lb[:, p * PAIR:(p + 1) * PAIR])
        bsum = _cumsum_rows(g, tri)
        tick(2)
        q = _silu(zq)
        og = _silu(zog)
        heads = [(b * HEADS + 2 * p + j, slice(j * HEAD_DIM, (j + 1) * HEAD_DIM))
                 for j in range(2)]
        scores = []
        for h, sl in heads:
            scores.append(_hgrn_scores(q[:, sl], k[:, sl], f[:, sl], bsum[:, sl], state_ref[h],
                                       sub, consts, tick))
            tick()
        tick(2)
        ys = []
        for (h, sl), (o_prev, blocks) in zip(heads, scores):
            o, new_state = _hgrn_output(o_prev, blocks, q[:, sl], k[:, sl], f[:, sl], zi[:, sl],
                                        bsum[:, sl], state_ref[h], consts[1][0])
            state_ref[h] = new_state
            ys.append((_rmsnorm(o, g_hgrn) * og[:, sl]).astype(BF16))
        return jnp.concatenate(ys, axis=1)

    def pool_branch(b, u):
        ext = jnp.concatenate([hist_ref[b], u], axis=0)
        hist_ref[b] = u[T - POOL_HISTORY:, :]
        acc = ext
        span = 1
        ys = []
        for gi, w in enumerate(POOL_WINDOWS):
            while span < w:
                acc = acc + pltpu.roll(acc, span, 0)
                span *= 2
            sl = slice(gi * POOL_GROUP_DIM, (gi + 1) * POOL_GROUP_DIM)
            pooled = acc[POOL_HISTORY:, sl] * (1.0 / w) - u[:, sl]
            y = _dot(pooled.astype(BF16), pool_w_ref[gi]) * pool_scale_ref[:, sl]
            ys.append(y.astype(BF16))
        return jnp.concatenate(ys, axis=1)

    n_blocks = D_MODEL // DENSE_COLS

    def enqueue_gated_b(b, z_pool, z_gb, z_ga):
        hold = {}
        queue.append((("side", b), lambda: hold.update(yb=pool_branch(b, assemble(z_pool)))))
        branch_b = enqueue_dot(("side", b), lambda: hold["yb"], w_b_ref, 0, D_MODEL)
        gate_a, gated_b = [], []

        def gates():
            gate_a.extend(_sigmoid(z_ga[cb]) for cb in range(n_blocks))
            gated_b.extend(_sigmoid(z_gb[cb]) * branch_b[cb] for cb in range(n_blocks))
        queue.append((("side", b), gates))
        return gate_a, gated_b

    def enqueue_outputs(b, ya, gate_a, gated_b):
        hold = {}
        queue.append((("out", b), lambda: hold.update(ya=jnp.concatenate(ya, axis=1))))
        branch_a = enqueue_dot(("out", b), lambda: hold["ya"], w_a_ref, 0, D_MODEL)

        def merge():
            merged = [gate_a[cb] * branch_a[cb] + gated_b[cb] for cb in range(n_blocks)]
            hold["merged"] = jnp.concatenate(merged, axis=1).astype(BF16)
        queue.append((("out", b), merge))
        for cb in range(n_blocks):
            def piece(cb=cb):
                cols = slice(cb * DENSE_COLS, (cb + 1) * DENSE_COLS)
                out_ref[b, :, cols] = xs[b][:, cols] + _dot(hold["merged"], w_out_ref[:, cols])
            queue.append((("out", b), piece))

    n_pairs = HEADS // 2
    order = [(b, p) for p in range(n_pairs) for b in range(B)]
    z = {order[0]: enqueue_pair(*order[0])}
    flush()
    for bp in order[1:]:
        z[bp] = enqueue_pair(*bp)
    others, sides = {}, {}
    for b in range(B):
        others[b] = tuple(enqueue_projection(("other", b), b, off, width) for off, width in
                          ((OFF_POOL, POOL_WIDTH), (OFF_GB, D_MODEL), (OFF_GA, D_MODEL)))
        sides[b] = enqueue_gated_b(b, *others[b])
    ya = {b: [] for b in range(B)}
    for b, p in order:
        flush(("pair", b, p))
        ya[b].append(hgrn_pair(b, p, *(assemble(blocks) for blocks in z[(b, p)])))
        if p == n_pairs - 1:
            enqueue_outputs(b, ya[b], *sides[b])
    flush()


def _ffn_kernel(h_ref, g_ffn_ref, w_gate_ref, w_up_ref, w_down_ref, g_final_ref, out_ref):
    groups = [slice(r * FFN_ROWS, (r + 1) * FFN_ROWS) for r in range(FFN_BLOCK // FFN_ROWS)]

    def normed(rows):
        return _rmsnorm(h_ref[rows, :], g_ffn_ref[...]).astype(BF16)

    def finish(rows, down):
        out_ref[rows, :] = _rmsnorm(h_ref[rows, :] + down, g_final_ref[...])

    def activation(n):
        d_ff = w_gate_ref.shape[1]
        acts = []
        for c0 in range(0, d_ff, DENSE_COLS):
            cols = slice(c0, c0 + DENSE_COLS)
            gate = _dot(n, w_gate_ref[:, cols])
            up = _dot(n, w_up_ref[:, cols])
            acts.append((_silu(gate) * up).astype(BF16))
        return jnp.concatenate(acts, axis=1)

    n = normed(groups[0])
    pending = None
    for i, rows in enumerate(groups):
        act = activation(n)
        if pending is not None:
            finish(*pending)
        if i + 1 < len(groups):
            n = normed(groups[i + 1])
        pending = (rows, _dot(act, w_down_ref[...]))
    finish(*pending)


def _resident(shape):
    zeros = (0,) * len(shape)
    return pl.BlockSpec(shape, lambda *_: zeros, pipeline_mode=pl.Buffered(1))


def _vmem_limit(resident, streamed):
    def nbytes(arrays):
        total = 0
        for shape, dtype in arrays:
            size = jnp.dtype(dtype).itemsize
            for dim in shape:
                size *= dim
            total += size
        return total

    limit = nbytes(resident) + 2 * nbytes(streamed) + VMEM_TEMPORARIES_BYTES
    assert limit <= VMEM_BYTES, "blocks and resident weights do not fit VMEM"
    return limit


def _row_slabs(rows, cols, steps):
    slab = next(r for r in range(BF16_ROWS, rows + 1, BF16_ROWS)
                if rows % r == 0 and rows // r <= steps)
    last = rows // slab - 1
    return pl.BlockSpec((slab, cols), lambda t: (jnp.minimum(t, last), 0))


def kernel(x, meta_tokens, norm_mix_g, w_in, lb_raw, hgrn_norm_g, pool_w, pool_scale,
           w_branch_a, w_branch_b, w_out, norm_ffn_g, w_ffn_gate, w_ffn_up, w_ffn_down,
           norm_final_g):
    B, S, D = x.shape
    assert D == D_MODEL and S % MIXER_BLOCK == 0 and (B * S) % FFN_BLOCK == 0
    assert w_in.shape == (1, D_MODEL, IN_WIDTH) and meta_tokens.shape == (N_META, D_MODEL)
    d_ff = w_ffn_gate.shape[-1]

    w_in_bf = w_in[0].astype(BF16)
    g_mix = norm_mix_g[0].reshape(1, D)
    lb_raw = lb_raw.astype(F32)

    def whole(shape):
        zeros = (0,) * len(shape)
        return pl.BlockSpec(shape, lambda i: zeros)

    def w_in_columns(offset, width):
        assert offset % width == 0
        return pl.BlockSpec((D, width), lambda i: (0, offset // width))

    state0, hist0 = pl.pallas_call(
        _meta_state_kernel,
        grid=(1,),
        in_specs=[whole((N_META, D)), whole((1, D)),
                  w_in_columns(OFF_F, HGRN_WIDTH), w_in_columns(OFF_I, HGRN_WIDTH),
                  w_in_columns(OFF_POOL, POOL_WIDTH), whole((2, HGRN_WIDTH))],
        out_specs=(whole((HEADS, HEAD_DIM, HEAD_DIM)), whole((N_META, POOL_WIDTH))),
        out_shape=(jax.ShapeDtypeStruct((HEADS, HEAD_DIM, HEAD_DIM), F32),
                   jax.ShapeDtypeStruct((N_META, POOL_WIDTH), F32)),
        compiler_params=pltpu.CompilerParams(vmem_limit_bytes=_vmem_limit(
            [], [((D, 2 * HGRN_WIDTH + POOL_WIDTH), BF16), ((N_META, D), F32)])),
        name="meta_state",
    )(meta_tokens, g_mix, w_in_bf, w_in_bf, w_in_bf, lb_raw)

    T = MIXER_BLOCK
    steps = S // T
    ffn_weight_specs = [_row_slabs(D, d_ff, steps), _row_slabs(D, d_ff, steps),
                        _row_slabs(d_ff, D, steps)]
    h1, w_gate_bf, w_up_bf, w_down_bf = pl.pallas_call(
        _mixer_kernel,
        grid=(steps,),
        in_specs=[
            pl.BlockSpec((B, T, D), lambda t: (0, t, 0)),
            _resident((1, D)),
            _resident((D, IN_WIDTH)),
            _resident((2, HGRN_WIDTH)),
            _resident((1, HEAD_DIM)),
            _resident((len(POOL_WINDOWS), POOL_GROUP_DIM, POOL_GROUP_DIM)),
            _resident((1, POOL_WIDTH)),
            _resident((HGRN_WIDTH, D)),
            _resident((POOL_WIDTH, D)),
            _resident((D, D)),
            _resident((HEADS, HEAD_DIM, HEAD_DIM)),
            _resident((N_META, POOL_WIDTH)),
        ] + ffn_weight_specs,
        out_specs=[pl.BlockSpec((B, T, D), lambda t: (0, t, 0))] + ffn_weight_specs,
        out_shape=(jax.ShapeDtypeStruct((B, S, D), F32),
                   jax.ShapeDtypeStruct((D, d_ff), BF16),
                   jax.ShapeDtypeStruct((D, d_ff), BF16),
                   jax.ShapeDtypeStruct((d_ff, D), BF16)),
        scratch_shapes=[
            pltpu.VMEM((B * HEADS, HEAD_DIM, HEAD_DIM), F32),
            pltpu.VMEM((B, POOL_HISTORY, POOL_WIDTH), F32),
        ],
        compiler_params=pltpu.CompilerParams(
            dimension_semantics=("arbitrary",),
            vmem_limit_bytes=_vmem_limit(
                [((D, IN_WIDTH + HGRN_WIDTH + POOL_WIDTH + D), BF16),
                 (((B + 1) * HEADS, HEAD_DIM, HEAD_DIM), F32)],
                [((B, T, D), F32)] * 2
                + [(spec.block_shape, dtype) for spec in ffn_weight_specs
                   for dtype in (F32, BF16)])),
        name="mixer",
    )(x, g_mix, w_in_bf, lb_raw, hgrn_norm_g[0].reshape(1, HEAD_DIM),
      pool_w[0].astype(BF16), pool_scale[0].reshape(1, POOL_WIDTH),
      w_branch_a[0].astype(BF16), w_branch_b[0].astype(BF16), w_out[0].astype(BF16),
      state0, hist0, w_ffn_gate[0], w_ffn_up[0], w_ffn_down[0])

    TM = FFN_BLOCK
    out = pl.pallas_call(
        _ffn_kernel,
        grid=(B * S // TM,),
        in_specs=[
            pl.BlockSpec((TM, D), lambda i: (i, 0)),
            _resident((1, D)),
            _resident((D, d_ff)),
            _resident((D, d_ff)),
            _resident((d_ff, D)),
            _resident((1, D)),
        ],
        out_specs=pl.BlockSpec((TM, D), lambda i: (i, 0)),
        out_shape=jax.ShapeDtypeStruct((B * S, D), F32),
        compiler_params=pltpu.CompilerParams(
            dimension_semantics=("arbitrary",),
            vmem_limit_bytes=_vmem_limit([((D, 3 * d_ff), BF16)], [((TM, D), F32)] * 2)),
        name="ffn",
    )(h1.reshape(B * S, D), norm_ffn_g[0].reshape(1, D), w_gate_bf, w_up_bf, w_down_bf,
      norm_final_g.reshape(1, D))
    return out.reshape(B, S, D)
```
